```python
import math
import jax, jax.numpy as jnp
from jax import lax
import numpy as np

D_MODEL = 1024
BATCH = 8
SEQ = 4096
DEPTH = 2

N_MIXERS = 2
BLOCK_Q = 128
DA_HEADS = 8
DA_HEAD_DIM = 64
SB_HEADS = 16
SB_HEAD_DIM = 64
REL_BUCKETS = 32
REL_MAX_DIST = 128
MOE_GROUPS = 4
MOE_EXPERTS_PER_GROUP = 8
MOE_TOP_K = 2
MOE_HIDDEN = 512
N_EXPERTS = MOE_GROUPS * MOE_EXPERTS_PER_GROUP
DEEPNORM_ALPHA = (2 * DEPTH) ** 0.25
DEEPNORM_BETA = (8 * DEPTH) ** -0.25
LN_EPS = 1e-5
RMS_EPS = 1e-6
NEG_INF = -1e30
N_DA_LAYERS = (DEPTH + 1) // 2
N_SB_LAYERS = DEPTH // 2

kernel_name = "hybrid_diffattn_stickbreak_hmoe_deepnorm"


def layer_norm(x, g, b):
    xf = x.astype(jnp.float32)
    mu = jnp.mean(xf, axis=-1, keepdims=True)
    xc = xf - mu
    var = jnp.mean(xc * xc, axis=-1, keepdims=True)
    return (xc * lax.rsqrt(var + LN_EPS) * g.astype(jnp.float32) + b.astype(jnp.float32)).astype(x.dtype)


def rel_bucket(q_pos, k_pos):
    n = jnp.maximum(q_pos[:, None] - k_pos[None, :], 0)
    max_exact = REL_BUCKETS // 2
    nf = jnp.maximum(n, 1).astype(jnp.float32)
    large = max_exact + (jnp.log(nf / max_exact) / math.log(REL_MAX_DIST / max_exact)
                         * (REL_BUCKETS - max_exact)).astype(jnp.int32)
    large = jnp.minimum(large, REL_BUCKETS - 1)
    return jnp.where(n < max_exact, n, large)


def diff_attention(x, wq, wk, wv, wo, lq1, lk1, lq2, lk2, subln_g, rel_table, layer_idx):
    b, s, _ = x.shape
    q = (x @ wq).reshape(b, s, DA_HEADS, 2, DA_HEAD_DIM)
    k = (x @ wk).reshape(b, s, DA_HEADS, 2, DA_HEAD_DIM)
    v = (x @ wv).reshape(b, s, DA_HEADS, 2 * DA_HEAD_DIM)
    lam_init = 0.8 - 0.6 * math.exp(-0.3 * layer_idx)
    f32 = jnp.float32
    lam = (jnp.exp(jnp.sum(lq1.astype(f32) * lk1.astype(f32)))
           - jnp.exp(jnp.sum(lq2.astype(f32) * lk2.astype(f32))) + lam_init)
    k_pos = jnp.arange(s)
    scale = DA_HEAD_DIM ** -0.5
    g = subln_g.astype(f32)

    def block(i):
        start = i * BLOCK_Q
        qb = lax.dynamic_slice_in_dim(q, start, BLOCK_Q, axis=1)
        q_pos = start + jnp.arange(BLOCK_Q)
        bias = jnp.transpose(rel_table[rel_bucket(q_pos, k_pos)], (2, 0, 1)).astype(f32)
        logits = jnp.einsum('bqhcd,bkhcd->bhcqk', qb, k).astype(f32) * scale + bias[None, :, None]
        causal = k_pos[None, :] <= q_pos[:, None]
        logits = jnp.where(causal, logits, NEG_INF)
        p = jax.nn.softmax(logits, axis=-1)
        attn = p[:, :, 0] - lam * p[:, :, 1]
        o = jnp.einsum('bhqk,bkhe->bqhe', attn.astype(x.dtype), v).astype(f32)
        o = o * lax.rsqrt(jnp.mean(o * o, axis=-1, keepdims=True) + RMS_EPS) * g
        return (o * (1.0 - lam_init)).astype(x.dtype)

    blocks = lax.map(block, jnp.arange(s // BLOCK_Q))
    o = jnp.moveaxis(blocks, 0, 1).reshape(b, s, DA_HEADS * 2 * DA_HEAD_DIM)
    return o @ wo


def stick_breaking(x, wq, wk, wv, wo):
    b, s, _ = x.shape
    q = (x @ wq).reshape(b, s, SB_HEADS, SB_HEAD_DIM)
    k = (x @ wk).reshape(b, s, SB_HEADS, SB_HEAD_DIM)
    v = (x @ wv).reshape(b, s, SB_HEADS, SB_HEAD_DIM)
    k_pos = jnp.arange(s)
    scale = SB_HEAD_DIM ** -0.5

    def block(i):
        start = i * BLOCK_Q
        qb = lax.dynamic_slice_in_dim(q, start, BLOCK_Q, axis=1)
        q_pos = start + jnp.arange(BLOCK_Q)
        z = jnp.einsum('bqhd,bkhd->bhqk', qb, k).astype(jnp.float32) * scale
        strict = k_pos[None, :] < q_pos[:, None]
        log_fail = jnp.where(strict, jax.nn.log_sigmoid(-z), 0.0)
        later = lax.cumsum(log_fail, axis=3, reverse=True) - log_fail
        w = jnp.where(strict, jnp.exp(jax.nn.log_sigmoid(z) + later), 0.0)
        return jnp.einsum('bhqk,bkhd->bqhd', w.astype(x.dtype), v)

    blocks = lax.map(block, jnp.arange(s // BLOCK_Q))
    o = jnp.moveaxis(blocks, 0, 1).reshape(b, s, SB_HEADS * SB_HEAD_DIM)
    return o @ wo


def hier_moe(x, w_group, b_group, w_expert, b_expert, w1, w3, w2):
    b, s, d = x.shape
    t = b * s
    f32 = jnp.float32
    xt = x.reshape(t, d)
    g_prob = jax.nn.softmax((xt @ w_group + b_group).astype(f32), axis=-1)
    g_gate, g_idx = lax.top_k(g_prob, 1)
    e_logits = (xt @ w_expert + b_expert).astype(f32).reshape(t, MOE_GROUPS, MOE_EXPERTS_PER_GROUP)
    e_logits = jnp.take_along_axis(e_logits, g_idx[:, :, None], axis=1)[:, 0]
    e_prob = jax.nn.softmax(e_logits, axis=-1)
    e_gate, e_idx = lax.top_k(e_prob, MOE_TOP_K)
    e_gate = e_gate / jnp.sum(e_gate, axis=-1, keepdims=True)
    weight = g_gate * e_gate
    expert_id = g_idx * MOE_EXPERTS_PER_GROUP + e_idx
    gates = jnp.sum(jax.nn.one_hot(expert_id, N_EXPERTS, dtype=f32) * weight[..., None], axis=1)

    def step(acc, ew):
        w1e, w3e, w2e, ge = ew
        h = jax.nn.silu(xt @ w1e) * (xt @ w3e)
        return acc + ge[:, None] * (h @ w2e).astype(f32), None

    y, _ = lax.scan(step, jnp.zeros((t, d), f32), (w1, w3, w2, gates.T))
    return y.astype(x.dtype).reshape(b, s, d)


def setup_inputs(seed: int = 0) -> dict:
    key = jax.random.key(seed)
    ks = jax.random.split(key, 26)
    nrm = jax.random.normal
    D = D_MODEL
    da_qk = DA_HEADS * 2 * DA_HEAD_DIM
    da_v = DA_HEADS * 2 * DA_HEAD_DIM
    sb_w = SB_HEADS * SB_HEAD_DIM
    inv = D ** -0.5
    return {
        "x": nrm(ks[0], (BATCH, SEQ, D), jnp.float32),
        "rel_table": 0.5 * nrm(ks[1], (REL_BUCKETS, DA_HEADS), jnp.float32),
        "da_wq": inv * nrm(ks[2], (N_DA_LAYERS, D, da_qk), jnp.float32),
        "da_wk": inv * nrm(ks[3], (N_DA_LAYERS, D, da_qk), jnp.float32),
        "da_wv": DEEPNORM_BETA * inv * nrm(ks[4], (N_DA_LAYERS, D, da_v), jnp.float32),
        "da_wo": DEEPNORM_BETA * da_v ** -0.5 * nrm(ks[5], (N_DA_LAYERS, da_v, D), jnp.float32),
        "da_lq1": 0.1 * nrm(ks[6], (N_DA_LAYERS, DA_HEAD_DIM), jnp.float32),
        "da_lk1": 0.1 * nrm(ks[7], (N_DA_LAYERS, DA_HEAD_DIM), jnp.float32),
        "da_lq2": 0.1 * nrm(ks[8], (N_DA_LAYERS, DA_HEAD_DIM), jnp.float32),
        "da_lk2": 0.1 * nrm(ks[9], (N_DA_LAYERS, DA_HEAD_DIM), jnp.float32),
        "da_subln_g": 1.0 + 0.02 * nrm(ks[10], (N_DA_LAYERS, 2 * DA_HEAD_DIM), jnp.float32),
        "sb_wq": inv * nrm(ks[11], (N_SB_LAYERS, D, sb_w), jnp.float32),
        "sb_wk": inv * nrm(ks[12], (N_SB_LAYERS, D, sb_w), jnp.float32),
        "sb_wv": DEEPNORM_BETA * inv * nrm(ks[13], (N_SB_LAYERS, D, sb_w), jnp.float32),
        "sb_wo": DEEPNORM_BETA * sb_w ** -0.5 * nrm(ks[14], (N_SB_LAYERS, sb_w, D), jnp.float32),
        "ln_mix_g": 1.0 + 0.02 * nrm(ks[15], (DEPTH, D), jnp.float32),
        "ln_mix_b": 0.02 * nrm(ks[16], (DEPTH, D), jnp.float32),
        "ln_ffn_g": 1.0 + 0.02 * nrm(ks[17], (DEPTH, D), jnp.float32),
        "ln_ffn_b": 0.02 * nrm(ks[18], (DEPTH, D), jnp.float32),
        "moe_w_group": inv * nrm(ks[19], (DEPTH, D, MOE_GROUPS), jnp.float32),
        "moe_b_group": 0.01 * nrm(ks[20], (DEPTH, MOE_GROUPS), jnp.float32),
        "moe_w_expert": inv * nrm(ks[21], (DEPTH, D, N_EXPERTS), jnp.float32),
        "moe_b_expert": 0.01 * nrm(ks[22], (DEPTH, N_EXPERTS), jnp.float32),
        "moe_w1": inv * nrm(ks[23], (DEPTH, N_EXPERTS, D, MOE_HIDDEN), jnp.float32),
        "moe_w3": inv * nrm(ks[24], (DEPTH, N_EXPERTS, D, MOE_HIDDEN), jnp.float32),
        "moe_w2": DEEPNORM_BETA * MOE_HIDDEN ** -0.5 * nrm(ks[25], (DEPTH, N_EXPERTS, MOE_HIDDEN, D), jnp.float32),
    }


def reference(x, rel_table, da_wq, da_wk, da_wv, da_wo, da_lq1, da_lk1, da_lq2, da_lk2, da_subln_g,
              sb_wq, sb_wk, sb_wv, sb_wo, ln_mix_g, ln_mix_b, ln_ffn_g, ln_ffn_b,
              moe_w_group, moe_b_group, moe_w_expert, moe_b_expert, moe_w1, moe_w3, moe_w2):
    for layer in range(DEPTH):
        j = layer // N_MIXERS
        if layer % N_MIXERS == 0:
            mix = diff_attention(x, da_wq[j], da_wk[j], da_wv[j], da_wo[j], da_lq1[j], da_lk1[j],
                                 da_lq2[j], da_lk2[j], da_subln_g[j], rel_table, layer)
        else:
            mix = stick_breaking(x, sb_wq[j], sb_wk[j], sb_wv[j], sb_wo[j])
        x = layer_norm(DEEPNORM_ALPHA * x + mix, ln_mix_g[layer], ln_mix_b[layer])
        ffn = hier_moe(x, moe_w_group[layer], moe_b_group[layer], moe_w_expert[layer],
                       moe_b_expert[layer], moe_w1[layer], moe_w3[layer], moe_w2[layer])
        x = layer_norm(DEEPNORM_ALPHA * x + ffn, ln_ffn_g[layer], ln_ffn_b[layer])
    return x
```

```python
import functools
import math

import jax
import jax.numpy as jnp
from jax import lax
from jax.experimental import pallas as pl
from jax.experimental.pallas import tpu as pltpu

F32 = jnp.float32
BF16 = jnp.bfloat16
I32 = jnp.int32

DEPTH = 2
DA_HEADS = 8
DA_HEAD_DIM = 64
SB_HEAD_DIM = 64
REL_BUCKETS = 32
REL_MAX_DIST = 128
MOE_GROUPS = 4
MOE_EXPERTS_PER_GROUP = 8
N_EXPERTS = MOE_GROUPS * MOE_EXPERTS_PER_GROUP
DEEPNORM_ALPHA = (2 * DEPTH) ** 0.25
LN_EPS = 1e-5
RMS_EPS = 1e-6
NEG_INF = -1e30
LOG2E = 1.4426950408889634

LANES = 128
ATT_BLOCK = 256
ROW_TILE = 512
FFN_TILE = 256
VMEM_LIMIT = 48 * 1024 * 1024


def _params(n_axes, vmem=VMEM_LIMIT):
    return pltpu.CompilerParams(dimension_semantics=("arbitrary",) * n_axes, vmem_limit_bytes=vmem)


def _proj_kernel(x_ref, wq_ref, wk_ref, wv_ref, q_ref, k_ref, v_ref, *, q_scale):
    x = x_ref[...].astype(BF16)
    q = jnp.dot(x, wq_ref[...], preferred_element_type=F32)
    q_ref[...] = (q * q_scale).astype(BF16)
    k_ref[...] = jnp.dot(x, wk_ref[...], preferred_element_type=F32).astype(BF16)
    v_ref[...] = jnp.dot(x, wv_ref[...], preferred_element_type=F32).astype(BF16)


def _qkv_proj(x2d, wq, wk, wv, q_scale):
    t, d = x2d.shape
    n = wq.shape[1]
    tm = min(ROW_TILE, t)
    row_in = pl.BlockSpec((tm, d), lambda i: (i, 0))
    w_spec = pl.BlockSpec((d, n), lambda i: (0, 0))
    row_out = pl.BlockSpec((tm, n), lambda i: (i, 0))
    return pl.pallas_call(
        functools.partial(_proj_kernel, q_scale=q_scale),
        grid=(t // tm,),
        in_specs=[row_in, w_spec, w_spec, w_spec],
        out_specs=[row_out, row_out, row_out],
        out_shape=[jax.ShapeDtypeStruct((t, n), BF16)] * 3,
        compiler_params=_params(1),
        name="qkv_proj",
    )(x2d, wq.astype(BF16), wk.astype(BF16), wv.astype(BF16))


def _bias_kernel(tab_ref, out_ref, *, blk):
    h = pl.program_id(0)
    i = lax.broadcasted_iota(I32, (blk, 2 * blk), 0)
    j = lax.broadcasted_iota(I32, (blk, 2 * blk), 1)
    n = blk + i - j
    nn = jnp.maximum(n, 0)
    max_exact = REL_BUCKETS // 2
    nf = jnp.maximum(nn, 1).astype(F32)
    large = max_exact + (jnp.log(nf * (1.0 / max_exact)) * ((REL_BUCKETS - max_exact) / math.log(REL_MAX_DIST / max_exact))
                         ).astype(I32)
    large = jnp.minimum(large, REL_BUCKETS - 1)
    bucket = jnp.where(nn < max_exact, nn, large)
    far = tab_ref[REL_BUCKETS - 1, h]
    acc = jnp.zeros((blk, 2 * blk), F32)
    for b in range(REL_BUCKETS):
        acc = jnp.where(bucket == b, tab_ref[b, h] - far, acc)
    out_ref[0] = jnp.where(n >= 0, acc * LOG2E, NEG_INF)


def _bias_tiles(rel_table, blk):
    heads = rel_table.shape[1]
    return pl.pallas_call(
        functools.partial(_bias_kernel, blk=blk),
        grid=(heads,),
        in_specs=[pl.BlockSpec(memory_space=pltpu.SMEM)],
        out_specs=pl.BlockSpec((1, blk, 2 * blk), lambda h: (h, 0, 0)),
        out_shape=jax.ShapeDtypeStruct((heads, blk, 2 * blk), F32),
        compiler_params=_params(1),
        name="rel_bias_tiles",
    )(rel_table.astype(F32))


def _split_streams(q):
    lane = lax.broadcasted_iota(I32, q.shape, 1)
    zero = jnp.zeros_like(q)
    return jnp.concatenate([jnp.where(lane < 64, q, zero), jnp.where(lane >= 64, q, zero)], axis=0)


def _da_kernel(q_ref, k_ref, v_ref, bias_ref, lq1_ref, lk1_ref, lq2_ref, lk2_ref, g_ref, o_ref,
               m_sc, l_sc, acc_sc, *, blk, lam_init):
    qi = pl.program_id(2)
    qs = _split_streams(q_ref[0])
    m_sc[...] = jnp.full(m_sc.shape, NEG_INF, F32)
    l_sc[...] = jnp.zeros(l_sc.shape, F32)
    acc_sc[...] = jnp.zeros(acc_sc.shape, F32)

    def block(ki, bias):
        start = pl.multiple_of(ki * blk, blk)
        kb = k_ref[0, pl.ds(start, blk), :]
        vb = v_ref[0, pl.ds(start, blk), :]
        s = lax.dot_general(qs, kb, (((1,), (1,)), ((), ())), preferred_element_type=F32)
        if bias is not None:
            s = s + jnp.concatenate([bias, bias], axis=0)
        m_prev = m_sc[...]
        m_next = jnp.maximum(m_prev, jnp.max(s, axis=1, keepdims=True))
        alpha = jnp.exp2(m_prev - m_next)
        p = jnp.exp2(s - jnp.concatenate([m_next] * (blk // LANES), axis=1))
        l_sc[...] = alpha * l_sc[...] + jnp.sum(p, axis=1, keepdims=True)
        acc_sc[...] = alpha * acc_sc[...] + jnp.dot(p.astype(BF16), vb, preferred_element_type=F32)
        m_sc[...] = m_next

    def far_body(ki, carry):
        block(ki, None)
        return carry

    lax.fori_loop(0, jnp.maximum(qi - 1, 0), far_body, 0)

    @pl.when(qi >= 1)
    def _():
        block(qi - 1, bias_ref[0, :, :blk])

    block(qi, bias_ref[0, :, blk:])

    lam = (jnp.exp(jnp.sum(lq1_ref[...] * lk1_ref[...], axis=1, keepdims=True))
           - jnp.exp(jnp.sum(lq2_ref[...] * lk2_ref[...], axis=1, keepdims=True)) + lam_init)
    inv_l = 1.0 / l_sc[...]
    o = acc_sc[:blk] * inv_l[:blk] - lam * (acc_sc[blk:] * inv_l[blk:])
    o = o * lax.rsqrt(jnp.mean(o * o, axis=1, keepdims=True) + RMS_EPS) * g_ref[...]
    o_ref[0] = (o * (1.0 - lam_init)).astype(BF16)


def _diff_attention(q, k, v, bias, lq1, lk1, lq2, lk2, subln_g, lam_init):
    b, s, _ = q.shape
    blk = ATT_BLOCK
    q_spec = pl.BlockSpec((1, blk, LANES), lambda bi, h, qi: (bi, qi, h))
    kv_spec = pl.BlockSpec((1, s, LANES), lambda bi, h, qi: (bi, 0, h))
    vec64 = pl.BlockSpec((1, DA_HEAD_DIM), lambda bi, h, qi: (0, 0))
    return pl.pallas_call(
        functools.partial(_da_kernel, blk=blk, lam_init=lam_init),
        grid=(b, DA_HEADS, s // blk),
        in_specs=[q_spec, kv_spec, kv_spec,
                  pl.BlockSpec((1, blk, 2 * blk), lambda bi, h, qi: (h, 0, 0)),
                  vec64, vec64, vec64, vec64,
                  pl.BlockSpec((1, LANES), lambda bi, h, qi: (0, 0))],
        out_specs=q_spec,
        out_shape=jax.ShapeDtypeStruct(q.shape, BF16),
        scratch_shapes=[pltpu.VMEM((2 * blk, LANES), F32)] * 3,
        compiler_params=_params(3),
        name="diff_attention",
    )(q, k, v, bias, lq1.reshape(1, -1), lk1.reshape(1, -1), lq2.reshape(1, -1), lk2.reshape(1, -1),
      subln_g.reshape(1, -1))


def _sb_kernel(q_ref, k_ref, v_ref, o_ref, r_sc, acc_sc, *, blk):
    qi = pl.program_id(2)
    rows = 2 * blk
    qs = _split_streams(q_ref[0])
    jj = lax.broadcasted_iota(I32, (blk, blk), 0)
    ss = lax.broadcasted_iota(I32, (blk, blk), 1)
    later = jnp.where(jj > ss, 1.0, 0.0).astype(BF16)
    lane = lax.broadcasted_iota(I32, (blk, LANES), 1)
    r_sc[...] = jnp.zeros(r_sc.shape, F32)
    acc_sc[...] = jnp.zeros(acc_sc.shape, F32)

    def block(ki, diagonal):
        start = pl.multiple_of(ki * blk, blk)
        kb = k_ref[0, pl.ds(start, blk), :]
        vb = v_ref[0, pl.ds(start, blk), :]
        zn = lax.dot_general(qs, kb, (((1,), (1,)), ((), ())), preferred_element_type=F32)
        softplus = jnp.log(1.0 + jnp.exp2(-jnp.abs(zn))) * LOG2E
        log_fail = jnp.minimum(zn, 0.0) - softplus
        log_beta = log_fail - zn
        if diagonal:
            r = lax.broadcasted_iota(I32, (rows, blk), 0)
            t = jnp.where(r >= blk, r - blk, r)
            strict = lax.broadcasted_iota(I32, (rows, blk), 1) < t
            log_fail = jnp.where(strict, log_fail, 0.0)
        hi = log_fail.astype(BF16)
        lo = (log_fail - hi.astype(F32)).astype(BF16)
        suffix = (jnp.dot(hi, later, preferred_element_type=F32)
                  + jnp.dot(lo, later, preferred_element_type=F32))
        w = jnp.exp2(log_beta + suffix)
        if diagonal:
            w = jnp.where(strict, w, 0.0)
        wb = w.astype(BF16)
        pv0 = jnp.dot(wb[:blk], vb, preferred_element_type=F32)
        pv1 = jnp.dot(wb[blk:], vb, preferred_element_type=F32)
        carry = jnp.exp2(r_sc[...])
        acc_sc[...] += jnp.where(lane < 64, carry[:blk] * pv0, carry[blk:] * pv1)
        r_sc[...] += suffix[:, 0:1] + log_fail[:, 0:1]

    block(qi, True)

    def body(step, carry):
        block(qi - 1 - step, False)
        return carry

    lax.fori_loop(0, qi, body, 0)
    o_ref[0] = acc_sc[...].astype(BF16)


def _stick_breaking(q, k, v):
    b, s, width = q.shape
    blk = ATT_BLOCK
    q_spec = pl.BlockSpec((1, blk, LANES), lambda bi, h, qi: (bi, qi, h))
    kv_spec = pl.BlockSpec((1, s, LANES), lambda bi, h, qi: (bi, 0, h))
    return pl.pallas_call(
        functools.partial(_sb_kernel, blk=blk),
        grid=(b, width // LANES, s // blk),
        in_specs=[q_spec, kv_spec, kv_spec],
        out_specs=q_spec,
        out_shape=jax.ShapeDtypeStruct(q.shape, BF16),
        scratch_shapes=[pltpu.VMEM((2 * blk, LANES), F32), pltpu.VMEM((blk, LANES), F32)],
        compiler_params=_params(3),
        name="stick_breaking",
    )(q, k, v)


def _layer_norm(y, g, b):
    mu = jnp.mean(y, axis=1, keepdims=True)
    yc = y - mu
    var = jnp.mean(yc * yc, axis=1, keepdims=True)
    return yc * lax.rsqrt(var + LN_EPS) * g + b


def _first_index_of_max(vals, lane_f):
    m = jnp.max(vals, axis=1, keepdims=True)
    idx = jnp.min(jnp.where(vals == m, lane_f, float(LANES)), axis=1, keepdims=True)
    return m, idx


def _mix_ln_route_kernel(o_ref, x_ref, wo_ref, g_ref, b_ref, wr_hi_ref, wr_lo_ref, br_ref,
                         x1_ref, mi_ref, mf_ref, cnt_ref, *, tm):
    step = pl.program_id(0)

    @pl.when(step == 0)
    def _():
        cnt_ref[...] = jnp.zeros(cnt_ref.shape, F32)

    mix = jnp.dot(o_ref[...], wo_ref[...], preferred_element_type=F32)
    x1 = _layer_norm(DEEPNORM_ALPHA * x_ref[...] + mix, g_ref[...], b_ref[...])
    x1_ref[...] = x1

    xh = x1.astype(BF16)
    xl = (x1 - xh.astype(F32)).astype(BF16)
    logits = (jnp.dot(xh, wr_hi_ref[...], preferred_element_type=F32)
              + jnp.dot(xh, wr_lo_ref[...], preferred_element_type=F32)
              + jnp.dot(xl, wr_hi_ref[...], preferred_element_type=F32)) + br_ref[...]
    lane = lax.broadcasted_iota(I32, (tm, LANES), 1)
    lane_f = lane.astype(F32)
    is_group = lane < MOE_GROUPS
    g_max, g_idx = _first_index_of_max(jnp.where(is_group, logits, NEG_INF), lane_f)
    g_gate = 1.0 / jnp.sum(jnp.where(is_group, jnp.exp(logits - g_max), 0.0), axis=1, keepdims=True)
    first = MOE_GROUPS + MOE_EXPERTS_PER_GROUP * g_idx
    in_group = (lane_f >= first) & (lane_f < first + MOE_EXPERTS_PER_GROUP)
    e_logits = jnp.where(in_group, logits, NEG_INF)
    m1, i1 = _first_index_of_max(e_logits, lane_f)
    m2, i2 = _first_index_of_max(jnp.where(lane_f == i1, NEG_INF, e_logits), lane_f)
    ratio = jnp.exp(m2 - m1)
    w1 = g_gate / (1.0 + ratio)
    w2 = w1 * ratio

    e1 = jnp.where(lane_f == i1, 1.0, 0.0)
    e2 = jnp.where(lane_f == i2, 1.0, 0.0)
    both = e1 + e2
    rr = lax.broadcasted_iota(I32, (tm, tm), 0)
    cc = lax.broadcasted_iota(I32, (tm, tm), 1)
    earlier = jnp.where(rr > cc, 1.0, 0.0).astype(BF16)
    before = cnt_ref[...] + jnp.dot(earlier, both.astype(BF16), preferred_element_type=F32)
    rank1 = jnp.sum(e1 * before, axis=1, keepdims=True)
    rank2 = jnp.sum(e2 * before, axis=1, keepdims=True)
    cnt_ref[...] += jnp.sum(both, axis=0, keepdims=True)

    meta = jnp.where(lane == 0, i1 - MOE_GROUPS,
                     jnp.where(lane == 1, i2 - MOE_GROUPS,
                               jnp.where(lane == 2, rank1, jnp.where(lane == 3, rank2, 0.0))))
    mi_ref[...] = meta.astype(I32)
    mf_ref[...] = jnp.where(lane == 0, w1, jnp.where(lane == 1, w2, 0.0))


def _mix_ln_route(o2d, x2d, wo, ln_g, ln_b, w_group, b_group, w_expert, b_expert):
    t, d = x2d.shape
    tm = min(ROW_TILE, t)
    pad = LANES - MOE_GROUPS - N_EXPERTS
    wr = jnp.pad(jnp.concatenate([w_group, w_expert], axis=1).astype(F32), ((0, 0), (0, pad)))
    wr_hi = wr.astype(BF16)
    wr_lo = (wr - wr_hi.astype(F32)).astype(BF16)
    br = jnp.pad(jnp.concatenate([b_group, b_expert]).astype(F32), (0, pad)).reshape(1, LANES)
    row = lambda n: pl.BlockSpec((tm, n), lambda i: (i, 0))
    full = lambda r, c: pl.BlockSpec((r, c), lambda i: (0, 0))
    return pl.pallas_call(
        functools.partial(_mix_ln_route_kernel, tm=tm),
        grid=(t // tm,),
        in_specs=[row(o2d.shape[1]), row(d), full(o2d.shape[1], d), full(1, d), full(1, d),
                  full(d, LANES), full(d, LANES), full(1, LANES)],
        out_specs=[row(d), row(LANES), row(LANES), full(1, LANES)],
        out_shape=[jax.ShapeDtypeStruct((t, d), F32), jax.ShapeDtypeStruct((t, LANES), I32),
                   jax.ShapeDtypeStruct((t, LANES), F32), jax.ShapeDtypeStruct((1, LANES), F32)],
        compiler_params=_params(1),
        name="mix_ln_route",
    )(o2d, x2d, wo.astype(BF16), ln_g.reshape(1, d).astype(F32), ln_b.reshape(1, d).astype(F32),
      wr_hi, wr_lo, br)


def _row_copy(src_ref, src_row, dst_ref, dst_row, sem):
    return pltpu.make_async_copy(src_ref.at[pl.ds(src_row, 1)], dst_ref.at[pl.ds(dst_row, 1)], sem)


def _dispatch_kernel(pos_ref, x_ref, xs_in_ref, xs_ref, sem, *, tm):
    del xs_in_ref

    def start(i, carry):
        for slot in range(2):
            _row_copy(x_ref, i, xs_ref, pos_ref[0, 0, 2 * i + slot], sem).start()
        return carry

    lax.fori_loop(0, tm, start, 0)

    def wait(i, carry):
        for slot in range(2):
            _row_copy(x_ref, 0, xs_ref, 0, sem).wait()
        return carry

    lax.fori_loop(0, tm, wait, 0)


def _dispatch(pos3, x1, n_rows):
    t, d = x1.shape
    tm = pos3.shape[2] // 2
    return pl.pallas_call(
        functools.partial(_dispatch_kernel, tm=tm),
        grid=(t // tm,),
        in_specs=[pl.BlockSpec((1, 1, 2 * tm), lambda i: (i, 0, 0), memory_space=pltpu.SMEM),
                  pl.BlockSpec((tm, d), lambda i: (i, 0)),
                  pl.BlockSpec(memory_space=pl.ANY)],
        out_specs=pl.BlockSpec(memory_space=pl.ANY),
        out_shape=jax.ShapeDtypeStruct((n_rows, d), F32),
        scratch_shapes=[pltpu.SemaphoreType.DMA(())],
        input_output_aliases={2: 0},
        compiler_params=_params(1),
        name="moe_dispatch",
    )(pos3, x1, jnp.zeros((n_rows, d), F32))


def _ffn_kernel(tile_expert_ref, n_used_ref, xs_ref, w1_ref, w3_ref, w2_ref, ys_ref):
    del tile_expert_ref
    i = pl.program_id(0)

    @pl.when(i < n_used_ref[0])
    def _():
        x = xs_ref[...].astype(BF16)
        a = jnp.dot(x, w1_ref[0], preferred_element_type=F32)
        b = jnp.dot(x, w3_ref[0], preferred_element_type=F32)
        h = (a * jax.nn.sigmoid(a) * b).astype(BF16)
        ys_ref[...] = jnp.dot(h, w2_ref[0], preferred_element_type=F32)

    @pl.when(i >= n_used_ref[0])
    def _():
        ys_ref[...] = jnp.zeros(ys_ref.shape, F32)


def _expert_ffn(tile_expert, n_used, xs, w1, w3, w2):
    n_rows, d = xs.shape
    hidden = w1.shape[2]
    tm = FFN_TILE
    x_spec = pl.BlockSpec((tm, d), lambda i, te, nu: (jnp.minimum(i, nu[0] - 1), 0))
    grid_spec = pltpu.PrefetchScalarGridSpec(
        num_scalar_prefetch=2,
        grid=(n_rows // tm,),
        in_specs=[x_spec,
                  pl.BlockSpec((1, d, hidden), lambda i, te, nu: (te[i], 0, 0)),
                  pl.BlockSpec((1, d, hidden), lambda i, te, nu: (te[i], 0, 0)),
                  pl.BlockSpec((1, hidden, d), lambda i, te, nu: (te[i], 0, 0))],
        out_specs=pl.BlockSpec((tm, d), lambda i, te, nu: (i, 0)),
    )
    return pl.pallas_call(
        _ffn_kernel,
        grid_spec=grid_spec,
        out_shape=jax.ShapeDtypeStruct((n_rows, d), F32),
        compiler_params=_params(1),
        name="moe_experts",
    )(tile_expert, n_used, xs, w1.astype(BF16), w3.astype(BF16), w2.astype(BF16))


def _combine_kernel(pos_ref, x1_ref, mf_ref, ys_ref, g_ref, b_ref, out_ref, ybuf, sem, *, tm):
    def start(i, carry):
        for slot in range(2):
            _row_copy(ys_ref, pos_ref[0, 0, 2 * i + slot], ybuf.at[slot], i, sem).start()
        return carry

    lax.fori_loop(0, tm, start, 0)

    def wait(i, carry):
        for slot in range(2):
            _row_copy(ys_ref, 0, ybuf.at[slot], 0, sem).wait()
        return carry

    lax.fori_loop(0, tm, wait, 0)
    gates = mf_ref[...]
    y = gates[:, 0:1] * ybuf[0] + gates[:, 1:2] * ybuf[1]
    out_ref[...] = _layer_norm(DEEPNORM_ALPHA * x1_ref[...] + y, g_ref[...], b_ref[...])


def _combine(pos3, x1, meta_f, ys, ln_g, ln_b):
    t, d = x1.shape
    tm = pos3.shape[2] // 2
    row = lambda n: pl.BlockSpec((tm, n), lambda i: (i, 0))
    full = pl.BlockSpec((1, d), lambda i: (0, 0))
    return pl.pallas_call(
        functools.partial(_combine_kernel, tm=tm),
        grid=(t // tm,),
        in_specs=[pl.BlockSpec((1, 1, 2 * tm), lambda i: (i, 0, 0), memory_space=pltpu.SMEM),
                  row(d), row(LANES), pl.BlockSpec(memory_space=pl.ANY), full, full],
        out_specs=row(d),
        out_shape=jax.ShapeDtypeStruct((t, d), F32),
        scratch_shapes=[pltpu.VMEM((2, tm, d), F32), pltpu.SemaphoreType.DMA(())],
        compiler_params=_params(1),
        name="moe_combine",
    )(pos3, x1, meta_f, ys, ln_g.reshape(1, d).astype(F32), ln_b.reshape(1, d).astype(F32))


def _routing_plan(meta_i, counts_row, n_tiles):
    counts = counts_row[0, MOE_GROUPS:MOE_GROUPS + N_EXPERTS].astype(I32)
    padded = ((counts + FFN_TILE - 1) // FFN_TILE) * FFN_TILE
    ends = jnp.cumsum(padded)
    offsets = ends - padded
    pos = offsets[meta_i[:, 0:2]] + meta_i[:, 2:4]
    tile_ends = ends // FFN_TILE
    tile_expert = jnp.minimum(
        jnp.searchsorted(tile_ends, jnp.arange(n_tiles, dtype=I32), side="right"), N_EXPERTS - 1).astype(I32)
    return pos.astype(I32), tile_expert, tile_ends[-1:].astype(I32)


def _hier_moe_ln(x1, meta_i, meta_f, counts_row, w1, w3, w2, ln_g, ln_b):
    t, d = x1.shape
    n_tiles = (2 * t) // FFN_TILE + N_EXPERTS
    pos, tile_expert, n_used = _routing_plan(meta_i, counts_row, n_tiles)
    tm = min(ROW_TILE, t)
    pos3 = pos.reshape(t // tm, 1, 2 * tm)
    xs = _dispatch(pos3, x1, n_tiles * FFN_TILE)
    ys = _expert_ffn(tile_expert, n_used, xs, w1, w3, w2)
    return _combine(pos3, x1, meta_f, ys, ln_g, ln_b)


def kernel(x, rel_table, da_wq, da_wk, da_wv, da_wo, da_lq1, da_lk1, da_lq2, da_lk2, da_subln_g, sb_wq, sb_wk,
           sb_wv, sb_wo, ln_mix_g, ln_mix_b, ln_ffn_g, ln_ffn_b, moe_w_group, moe_b_group, moe_w_expert,
           moe_b_expert, moe_w1, moe_w3, moe_w2):
    b, s, d = x.shape
    t = b * s
    assert s % ATT_BLOCK == 0 and t % min(ROW_TILE, t) == 0 and (2 * t) % FFN_TILE == 0
    xf = x.reshape(t, d).astype(F32)
    bias = _bias_tiles(rel_table, ATT_BLOCK)
    for layer in range(DEPTH):
        j = layer // 2
        if layer % 2 == 0:
            q, k, v = _qkv_proj(xf, da_wq[j], da_wk[j], da_wv[j], DA_HEAD_DIM ** -0.5 * LOG2E)
            shape = (b, s, q.shape[1])
            lam_init = 0.8 - 0.6 * math.exp(-0.3 * layer)
            o = _diff_attention(q.reshape(shape), k.reshape(shape), v.reshape(shape), bias,
                                da_lq1[j], da_lk1[j], da_lq2[j], da_lk2[j], da_subln_g[j], lam_init)
            wo = da_wo[j]
        else:
            q, k, v = _qkv_proj(xf, sb_wq[j], sb_wk[j], sb_wv[j], -(SB_HEAD_DIM ** -0.5) * LOG2E)
            shape = (b, s, q.shape[1])
            o = _stick_breaking(q.reshape(shape), k.reshape(shape), v.reshape(shape))
            wo = sb_wo[j]
        x1, meta_i, meta_f, counts = _mix_ln_route(
            o.reshape(t, -1), xf, wo, ln_mix_g[layer], ln_mix_b[layer],
            moe_w_group[layer], moe_b_group[layer], moe_w_expert[layer], moe_b_expert[layer])
        xf = _hier_moe_ln(x1, meta_i, meta_f, counts, moe_w1[layer], moe_w3[layer], moe_w2[layer],
                          ln_ffn_g[layer], ln_ffn_b[layer])
    return xf.reshape(b, s, d).astype(x.dtype)
```

```python
import functools
import math

import jax
import jax.numpy as jnp
from jax import lax
from jax.experimental import pallas as pl
from jax.experimental.pallas import tpu as pltpu

F32 = jnp.float32
BF16 = jnp.bfloat16
I32 = jnp.int32

DEPTH = 2
DA_HEADS = 8
DA_HEAD_DIM = 64
SB_HEAD_DIM = 64
REL_BUCKETS = 32
REL_MAX_DIST = 128
MOE_GROUPS = 4
MOE_EXPERTS_PER_GROUP = 8
N_EXPERTS = MOE_GROUPS * MOE_EXPERTS_PER_GROUP
DEEPNORM_ALPHA = (2 * DEPTH) ** 0.25
LN_EPS = 1e-5
RMS_EPS = 1e-6
NEG_INF = -1e30
LOG2E = 1.4426950408889634
SB_DEAD_LOG2 = -160.0

LANES = 128
DA_BLOCK = 512
DA_PAR = 2
SB_BLOCK = 256
SB_PAR = 4
ROW_TILE = 512
FFN_TILE = 256
META_ROWS = 8
ROW_GROUP = 8
VMEM_LIMIT = 48 * 1024 * 1024


def _params(n_axes, vmem=VMEM_LIMIT):
    return pltpu.CompilerParams(dimension_semantics=("arbitrary",) * n_axes, vmem_limit_bytes=vmem)


def _proj_kernel(x_ref, wq_ref, wk_ref, wv_ref, q_ref, k_ref, v_ref, *, q_scale):
    x = x_ref[...].astype(BF16)
    q = jnp.dot(x, wq_ref[...], preferred_element_type=F32)
    q_ref[...] = (q * q_scale).astype(BF16)
    k_ref[...] = jnp.dot(x, wk_ref[...], preferred_element_type=F32).astype(BF16)
    v_ref[...] = jnp.dot(x, wv_ref[...], preferred_element_type=F32).astype(BF16)


def _qkv_proj(x2d, wq, wk, wv, q_scale):
    t, d = x2d.shape
    n = wq.shape[1]
    tm = min(ROW_TILE, t)
    row_in = pl.BlockSpec((tm, d), lambda i: (i, 0))
    w_spec = pl.BlockSpec((d, n), lambda i: (0, 0))
    row_out = pl.BlockSpec((tm, n), lambda i: (i, 0))
    return pl.pallas_call(
        functools.partial(_proj_kernel, q_scale=q_scale),
        grid=(t // tm,),
        in_specs=[row_in, w_spec, w_spec, w_spec],
        out_specs=[row_out, row_out, row_out],
        out_shape=[jax.ShapeDtypeStruct((t, n), BF16)] * 3,
        compiler_params=_params(1),
        name="qkv_proj",
    )(x2d, wq.astype(BF16), wk.astype(BF16), wv.astype(BF16))


def _bias_kernel(tab_ref, out_ref, *, blk):
    h = pl.program_id(0)
    i = lax.broadcasted_iota(I32, (blk, 2 * blk), 0)
    j = lax.broadcasted_iota(I32, (blk, 2 * blk), 1)
    n = blk + i - j
    nn = jnp.maximum(n, 0)
    max_exact = REL_BUCKETS // 2
    nf = jnp.maximum(nn, 1).astype(F32)
    large = max_exact + (jnp.log(nf * (1.0 / max_exact)) * ((REL_BUCKETS - max_exact) / math.log(REL_MAX_DIST / max_exact))
                         ).astype(I32)
    large = jnp.minimum(large, REL_BUCKETS - 1)
    bucket = jnp.where(nn < max_exact, nn, large)
    far = tab_ref[REL_BUCKETS - 1, h]
    acc = jnp.zeros((blk, 2 * blk), F32)
    for b in range(REL_BUCKETS):
        acc = jnp.where(bucket == b, tab_ref[b, h] - far, acc)
    out_ref[0] = jnp.where(n >= 0, acc * LOG2E, NEG_INF)


def _bias_tiles(rel_table, blk):
    heads = rel_table.shape[1]
    return pl.pallas_call(
        functools.partial(_bias_kernel, blk=blk),
        grid=(heads,),
        in_specs=[pl.BlockSpec(memory_space=pltpu.SMEM)],
        out_specs=pl.BlockSpec((1, blk, 2 * blk), lambda h: (h, 0, 0)),
        out_shape=jax.ShapeDtypeStruct((heads, blk, 2 * blk), F32),
        compiler_params=_params(1),
        name="rel_bias_tiles",
    )(rel_table.astype(F32))


def _split_streams(q):
    lane = lax.broadcasted_iota(I32, q.shape, 1)
    zero = jnp.zeros_like(q)
    return jnp.concatenate([jnp.where(lane < 64, q, zero), jnp.where(lane >= 64, q, zero)], axis=0)


def _da_kernel(q_ref, k_ref, v_ref, bias_ref, lq1_ref, lk1_ref, lq2_ref, lk2_ref, g_ref, o_ref,
               m_sc, l_sc, acc_sc, s_sc, *, blk, lam_init, n_par):
    qi = pl.program_id(2)
    heads = range(n_par)
    qs = [_split_streams(q_ref[0, :, c * LANES:(c + 1) * LANES]) for c in heads]
    m_sc[...] = jnp.full(m_sc.shape, NEG_INF, F32)
    l_sc[...] = jnp.zeros(l_sc.shape, F32)
    acc_sc[...] = jnp.zeros(acc_sc.shape, F32)

    def logits(ki):
        start = pl.multiple_of(ki * blk, blk)
        return [lax.dot_general(qs[c], k_ref[0, pl.ds(start, blk), c * LANES:(c + 1) * LANES],
                                (((1,), (1,)), ((), ())), preferred_element_type=F32) for c in heads]

    def consume(ki, bias_cols):
        start = pl.multiple_of(ki * blk, blk)
        alpha, p = [], []
        for c in heads:
            sc = s_sc[c]
            if bias_cols is not None:
                bias = bias_ref[c, :, bias_cols]
                sc = sc + jnp.concatenate([bias, bias], axis=0)
            m_prev = m_sc[c]
            m_next = jnp.maximum(m_prev, jnp.max(sc, axis=1, keepdims=True))
            m_sc[c] = m_next
            alpha.append(jnp.exp2(m_prev - m_next))
            pc = jnp.exp2(sc - jnp.concatenate([m_next] * (blk // LANES), axis=1))
            l_sc[c] = alpha[c] * l_sc[c] + sum(pc[:, n * LANES:(n + 1) * LANES] for n in range(blk // LANES))
            p.append(pc.astype(BF16))
        pv = [jnp.dot(p[c], v_ref[0, pl.ds(start, blk), c * LANES:(c + 1) * LANES],
                      preferred_element_type=F32) for c in heads]
        for c in heads:
            acc_sc[c] = alpha[c] * acc_sc[c] + pv[c]

    def step(ki, bias_cols):
        nxt = logits(ki + 1)
        consume(ki, bias_cols)
        for c in heads:
            s_sc[c] = nxt[c]

    first = logits(0)
    for c in heads:
        s_sc[c] = first[c]

    def far_body(ki, carry):
        step(ki, None)
        return carry

    lax.fori_loop(0, jnp.maximum(qi - 1, 0), far_body, 0)

    @pl.when(qi >= 1)
    def _():
        step(qi - 1, slice(0, blk))

    consume(qi, slice(blk, 2 * blk))

    lam = (jnp.exp(jnp.sum(lq1_ref[...] * lk1_ref[...], axis=1, keepdims=True))
           - jnp.exp(jnp.sum(lq2_ref[...] * lk2_ref[...], axis=1, keepdims=True)) + lam_init)
    for c in heads:
        inv_l = 1.0 / jnp.sum(l_sc[c], axis=1, keepdims=True)
        acc = acc_sc[c]
        o = acc[:blk] * inv_l[:blk] - lam * (acc[blk:] * inv_l[blk:])
        o = o * lax.rsqrt(jnp.mean(o * o, axis=1, keepdims=True) + RMS_EPS) * g_ref[...]
        o_ref[0, :, c * LANES:(c + 1) * LANES] = (o * (1.0 - lam_init)).astype(BF16)


def _diff_attention(q, k, v, bias, lq1, lk1, lq2, lk2, subln_g, lam_init):
    b, s, _ = q.shape
    blk, n_par = DA_BLOCK, DA_PAR
    q_spec = pl.BlockSpec((1, blk, n_par * LANES), lambda bi, h, qi: (bi, qi, h))
    kv_spec = pl.BlockSpec((1, s, n_par * LANES), lambda bi, h, qi: (bi, 0, h))
    vec64 = pl.BlockSpec((1, DA_HEAD_DIM), lambda bi, h, qi: (0, 0))
    state = pltpu.VMEM((n_par, 2 * blk, LANES), F32)
    return pl.pallas_call(
        functools.partial(_da_kernel, blk=blk, lam_init=lam_init, n_par=n_par),
        grid=(b, DA_HEADS // n_par, s // blk),
        in_specs=[q_spec, kv_spec, kv_spec,
                  pl.BlockSpec((n_par, blk, 2 * blk), lambda bi, h, qi: (h, 0, 0)),
                  vec64, vec64, vec64, vec64,
                  pl.BlockSpec((1, LANES), lambda bi, h, qi: (0, 0))],
        out_specs=q_spec,
        out_shape=jax.ShapeDtypeStruct(q.shape, BF16),
        scratch_shapes=[state, state, state, pltpu.VMEM((n_par, 2 * blk, blk), F32)],
        compiler_params=_params(3),
        name="diff_attention",
    )(q, k, v, bias, lq1.reshape(1, -1), lk1.reshape(1, -1), lq2.reshape(1, -1), lk2.reshape(1, -1),
      subln_g.reshape(1, -1))


def _sb_kernel(q_ref, k_ref, v_ref, o_ref, r_sc, acc_sc, *, blk, n_par):
    qi = pl.program_id(2)
    rows = 2 * blk
    qs = [_split_streams(q_ref[0, :, c * LANES:(c + 1) * LANES]) for c in range(n_par)]
    jj = lax.broadcasted_iota(I32, (blk, blk), 0)
    ss = lax.broadcasted_iota(I32, (blk, blk), 1)
    later = jnp.where(jj > ss, 1.0, 0.0).astype(BF16)
    lane = lax.broadcasted_iota(I32, (blk, LANES), 1)
    r_sc[...] = jnp.zeros(r_sc.shape, F32)
    acc_sc[...] = jnp.zeros(acc_sc.shape, F32)

    def blocks(ki, diagonal):
        start = pl.multiple_of(ki * blk, blk)
        pairs = range(n_par)
        zn = [lax.dot_general(qs[c], k_ref[0, pl.ds(start, blk), c * LANES:(c + 1) * LANES],
                              (((1,), (1,)), ((), ())), preferred_element_type=F32) for c in pairs]
        if diagonal:
            r = lax.broadcasted_iota(I32, (rows, blk), 0)
            t = jnp.where(r >= blk, r - blk, r)
            strict = lax.broadcasted_iota(I32, (rows, blk), 1) < t
        log_fail, log_beta, hi, lo = [], [], [], []
        for c in pairs:
            softplus = jnp.log(1.0 + jnp.exp2(-jnp.abs(zn[c]))) * LOG2E
            lf = jnp.minimum(zn[c], 0.0) - softplus
            log_beta.append(lf - zn[c])
            if diagonal:
                lf = jnp.where(strict, lf, 0.0)
            log_fail.append(lf)
            hi.append(lf.astype(BF16))
            lo.append((lf - hi[c].astype(F32)).astype(BF16))
        suffix = [jnp.dot(hi[c], later, preferred_element_type=F32)
                  + jnp.dot(lo[c], later, preferred_element_type=F32) for c in pairs]
        wb = []
        for c in pairs:
            w = jnp.exp2(log_beta[c] + suffix[c])
            if diagonal:
                w = jnp.where(strict, w, 0.0)
            wb.append(w.astype(BF16))
        pv = []
        for c in pairs:
            vb = v_ref[0, pl.ds(start, blk), c * LANES:(c + 1) * LANES]
            pv.append((jnp.dot(wb[c][:blk], vb, preferred_element_type=F32),
                       jnp.dot(wb[c][blk:], vb, preferred_element_type=F32)))
        for c in pairs:
            carry = jnp.exp2(r_sc[c])
            acc_sc[c] += jnp.where(lane < 64, carry[:blk] * pv[c][0], carry[blk:] * pv[c][1])
            r_sc[c] += suffix[c][:, 0:1] + log_fail[c][:, 0:1]

    blocks(qi, True)

    def alive():
        return jnp.max(r_sc[...]) > SB_DEAD_LOG2

    def cond(carry):
        step, live = carry
        return jnp.logical_and(step < qi, live)

    def body(carry):
        step, _ = carry
        blocks(qi - 1 - step, False)
        return step + 1, alive()

    lax.while_loop(cond, body, (jnp.int32(0), alive()))
    for c in range(n_par):
        o_ref[0, :, c * LANES:(c + 1) * LANES] = acc_sc[c].astype(BF16)


def _stick_breaking(q, k, v):
    b, s, width = q.shape
    blk, n_par = SB_BLOCK, SB_PAR
    q_spec = pl.BlockSpec((1, blk, n_par * LANES), lambda bi, h, qi: (bi, qi, h))
    kv_spec = pl.BlockSpec((1, s, n_par * LANES), lambda bi, h, qi: (bi, 0, h))
    return pl.pallas_call(
        functools.partial(_sb_kernel, blk=blk, n_par=n_par),
        grid=(b, width // (n_par * LANES), s // blk),
        in_specs=[q_spec, kv_spec, kv_spec],
        out_specs=q_spec,
        out_shape=jax.ShapeDtypeStruct(q.shape, BF16),
        scratch_shapes=[pltpu.VMEM((n_par, 2 * blk, LANES), F32), pltpu.VMEM((n_par, blk, LANES), F32)],
        compiler_params=_params(3),
        name="stick_breaking",
    )(q, k, v)


def _layer_norm(y, g, b):
    mu = jnp.mean(y, axis=1, keepdims=True)
    yc = y - mu
    var = jnp.mean(yc * yc, axis=1, keepdims=True)
    return yc * lax.rsqrt(var + LN_EPS) * g + b


def _first_index_of_max(vals, lane_f):
    m = jnp.max(vals, axis=1, keepdims=True)
    idx = jnp.min(jnp.where(vals == m, lane_f, float(LANES)), axis=1, keepdims=True)
    return m, idx


def _mix_ln_route_kernel(o_ref, x_ref, wo_ref, g_ref, b_ref, wr_hi_ref, wr_lo_ref, br_ref,
                         x1_ref, mi_ref, mf_ref, cnt_ref, *, tm):
    step = pl.program_id(0)

    @pl.when(step == 0)
    def _():
        cnt_ref[...] = jnp.zeros(cnt_ref.shape, F32)

    mix = jnp.dot(o_ref[...], wo_ref[...], preferred_element_type=F32)
    x1 = _layer_norm(DEEPNORM_ALPHA * x_ref[...] + mix, g_ref[...], b_ref[...])
    x1_ref[...] = x1

    xh = x1.astype(BF16)
    xl = (x1 - xh.astype(F32)).astype(BF16)
    logits = (jnp.dot(xh, wr_hi_ref[...], preferred_element_type=F32)
              + jnp.dot(xh, wr_lo_ref[...], preferred_element_type=F32)
              + jnp.dot(xl, wr_hi_ref[...], preferred_element_type=F32)) + br_ref[...]
    lane = lax.broadcasted_iota(I32, (tm, LANES), 1)
    lane_f = lane.astype(F32)
    is_group = lane < MOE_GROUPS
    g_max, g_idx = _first_index_of_max(jnp.where(is_group, logits, NEG_INF), lane_f)
    g_gate = 1.0 / jnp.sum(jnp.where(is_group, jnp.exp(logits - g_max), 0.0), axis=1, keepdims=True)
    first = MOE_GROUPS + MOE_EXPERTS_PER_GROUP * g_idx
    in_group = (lane_f >= first) & (lane_f < first + MOE_EXPERTS_PER_GROUP)
    e_logits = jnp.where(in_group, logits, NEG_INF)
    m1, i1 = _first_index_of_max(e_logits, lane_f)
    m2, i2 = _first_index_of_max(jnp.where(lane_f == i1, NEG_INF, e_logits), lane_f)
    ratio = jnp.exp(m2 - m1)
    w1 = g_gate / (1.0 + ratio)
    w2 = w1 * ratio

    e1 = jnp.where(lane_f == i1, 1.0, 0.0)
    e2 = jnp.where(lane_f == i2, 1.0, 0.0)
    both = e1 + e2
    rr = lax.broadcasted_iota(I32, (tm, tm), 0)
    cc = lax.broadcasted_iota(I32, (tm, tm), 1)
    earlier = jnp.where(rr > cc, 1.0, 0.0).astype(BF16)
    before = cnt_ref[...] + jnp.dot(earlier, both.astype(BF16), preferred_element_type=F32)
    rank1 = jnp.sum(e1 * before, axis=1, keepdims=True)
    rank2 = jnp.sum(e2 * before, axis=1, keepdims=True)
    cnt_ref[...] += jnp.sum(both, axis=0, keepdims=True)

    meta = jnp.where(lane == 0, i1 - MOE_GROUPS,
                     jnp.where(lane == 1, i2 - MOE_GROUPS,
                               jnp.where(lane == 2, rank1, jnp.where(lane == 3, rank2, 0.0))))
    mi_ref[...] = meta.T[:META_ROWS].astype(I32)
    mf_ref[...] = jnp.where(lane == 0, w1, jnp.where(lane == 1, w2, 0.0))


def _mix_ln_route(o2d, x2d, wo, ln_g, ln_b, w_group, b_group, w_expert, b_expert):
    t, d = x2d.shape
    tm = min(ROW_TILE, t)
    pad = LANES - MOE_GROUPS - N_EXPERTS
    wr = jnp.pad(jnp.concatenate([w_group, w_expert], axis=1).astype(F32), ((0, 0), (0, pad)))
    wr_hi = wr.astype(BF16)
    wr_lo = (wr - wr_hi.astype(F32)).astype(BF16)
    br = jnp.pad(jnp.concatenate([b_group, b_expert]).astype(F32), (0, pad)).reshape(1, LANES)
    row = lambda n: pl.BlockSpec((tm, n), lambda i: (i, 0))
    full = lambda r, c: pl.BlockSpec((r, c), lambda i: (0, 0))
    return pl.pallas_call(
        functools.partial(_mix_ln_route_kernel, tm=tm),
        grid=(t // tm,),
        in_specs=[row(o2d.shape[1]), row(d), full(o2d.shape[1], d), full(1, d), full(1, d),
                  full(d, LANES), full(d, LANES), full(1, LANES)],
        out_specs=[row(d), pl.BlockSpec((META_ROWS, tm), lambda i: (0, i)), row(LANES), full(1, LANES)],
        out_shape=[jax.ShapeDtypeStruct((t, d), F32), jax.ShapeDtypeStruct((META_ROWS, t), I32),
                   jax.ShapeDtypeStruct((t, LANES), F32), jax.ShapeDtypeStruct((1, LANES), F32)],
        compiler_params=_params(1),
        name="mix_ln_route",
    )(o2d, x2d, wo.astype(BF16), ln_g.reshape(1, d).astype(F32), ln_b.reshape(1, d).astype(F32),
      wr_hi, wr_lo, br)


def _row_copy(src_ref, src_row, dst_ref, dst_row, sem):
    return pltpu.make_async_copy(src_ref.at[pl.ds(src_row, 1)], dst_ref.at[pl.ds(dst_row, 1)], sem)


def _start_row_copies(tm, copy_of):
    def group(g, carry):
        base = pl.multiple_of(g * ROW_GROUP, ROW_GROUP)
        for r in range(ROW_GROUP):
            for slot in range(2):
                copy_of(base + r, slot).start(priority=slot)
        return carry

    lax.fori_loop(0, tm // ROW_GROUP, group, 0)


def _wait_row_copies(tm, copy_of):
    def group(g, carry):
        for _ in range(2 * ROW_GROUP):
            copy_of(0, 0).wait()
        return carry

    lax.fori_loop(0, tm // ROW_GROUP, group, 0)


def _dispatch_kernel(pad_base_ref, pad_len_ref, tail_ref, pos_ref, x_ref, xs_ref, zeros_sc, sem, *, tm, tile):
    @pl.when(pl.program_id(0) == 0)
    def _():
        zeros_sc[...] = jnp.zeros(zeros_sc.shape, F32)

        def pad_rows(e, carry):
            def one(r, c, wait):
                cp = _row_copy(zeros_sc, 0, xs_ref, pad_base_ref[e] + r, sem)
                cp.wait() if wait else cp.start()
                return c
            lax.fori_loop(0, pad_len_ref[e], functools.partial(one, wait=False), 0)
            lax.fori_loop(0, pad_len_ref[e], functools.partial(one, wait=True), 0)
            return carry

        lax.fori_loop(0, N_EXPERTS, pad_rows, 0)

        def tail_tile(i, carry):
            cp = pltpu.make_async_copy(zeros_sc, xs_ref.at[pl.ds(pl.multiple_of(i * tile, tile), tile)], sem)
            cp.start()
            cp.wait()
            return carry

        lax.fori_loop(tail_ref[0], tail_ref[1], tail_tile, 0)

    copy_of = lambda row, slot: _row_copy(x_ref, row, xs_ref, pos_ref[slot, row], sem)
    _start_row_copies(tm, copy_of)
    _wait_row_copies(tm, copy_of)


def _pos_spec(tm):
    return pl.BlockSpec((2, tm), lambda i, *prefetch: (0, i), memory_space=pltpu.SMEM)


def _dispatch(plan, x1, n_rows):
    t, d = x1.shape
    tm = min(ROW_TILE, t)
    grid_spec = pltpu.PrefetchScalarGridSpec(
        num_scalar_prefetch=3,
        grid=(t // tm,),
        in_specs=[_pos_spec(tm), pl.BlockSpec((tm, d), lambda i, *prefetch: (i, 0))],
        out_specs=pl.BlockSpec(memory_space=pl.ANY),
        scratch_shapes=[pltpu.VMEM((FFN_TILE, d), F32), pltpu.SemaphoreType.DMA(())],
    )
    return pl.pallas_call(
        functools.partial(_dispatch_kernel, tm=tm, tile=FFN_TILE),
        grid_spec=grid_spec,
        out_shape=jax.ShapeDtypeStruct((n_rows, d), F32),
        compiler_params=_params(1),
        name="moe_dispatch",
    )(plan["pad_base"], plan["pad_len"], plan["tail"], plan["pos"], x1)


def _ffn_kernel(tile_expert_ref, n_used_ref, xs_ref, w1_ref, w3_ref, w2_ref, ys_ref, w1_sc, w3_sc, w2_sc):
    i = pl.program_id(0)
    used = i < n_used_ref[0]
    new_expert = jnp.logical_or(i == 0, tile_expert_ref[i] != tile_expert_ref[jnp.maximum(i - 1, 0)])

    @pl.when(jnp.logical_and(used, new_expert))
    def _():
        w1_sc[...] = w1_ref[0, 0].astype(BF16)
        w3_sc[...] = w3_ref[0, 0].astype(BF16)
        w2_sc[...] = w2_ref[0, 0].astype(BF16)

    @pl.when(used)
    def _():
        x = xs_ref[...].astype(BF16)
        a = jnp.dot(x, w1_sc[...], preferred_element_type=F32)
        b = jnp.dot(x, w3_sc[...], preferred_element_type=F32)
        h = (a * jax.nn.sigmoid(a) * b).astype(BF16)
        ys_ref[...] = jnp.dot(h, w2_sc[...], preferred_element_type=F32)

    @pl.when(jnp.logical_not(used))
    def _():
        ys_ref[...] = jnp.zeros(ys_ref.shape, F32)


def _expert_ffn(plan, xs, w1, w3, w2, layer):
    n_rows, d = xs.shape
    hidden = w1.shape[3]
    tm = FFN_TILE
    x_spec = pl.BlockSpec((tm, d), lambda i, te, nu: (jnp.minimum(i, nu[0] - 1), 0))
    w_in = pl.BlockSpec((1, 1, d, hidden), lambda i, te, nu: (layer, te[i], 0, 0))
    w_out = pl.BlockSpec((1, 1, hidden, d), lambda i, te, nu: (layer, te[i], 0, 0))
    grid_spec = pltpu.PrefetchScalarGridSpec(
        num_scalar_prefetch=2,
        grid=(n_rows // tm,),
        in_specs=[x_spec, w_in, w_in, w_out],
        out_specs=pl.BlockSpec((tm, d), lambda i, te, nu: (i, 0)),
        scratch_shapes=[pltpu.VMEM((d, hidden), BF16), pltpu.VMEM((d, hidden), BF16),
                        pltpu.VMEM((hidden, d), BF16)],
    )
    return pl.pallas_call(
        _ffn_kernel,
        grid_spec=grid_spec,
        out_shape=jax.ShapeDtypeStruct((n_rows, d), F32),
        compiler_params=_params(1),
        name="moe_experts",
    )(plan["tile_expert"], plan["n_used"], xs, w1.astype(F32), w3.astype(F32), w2.astype(F32))


def _combine_kernel(pos_ref, x1_ref, mf_ref, ys_ref, g_ref, b_ref, out_ref, ybuf, sem, *, tm):
    copy_of = lambda row, slot: _row_copy(ys_ref, pos_ref[slot, row], ybuf.at[slot], row, sem)
    _start_row_copies(tm, copy_of)
    _wait_row_copies(tm, copy_of)
    gates = mf_ref[...]
    y = gates[:, 0:1] * ybuf[0] + gates[:, 1:2] * ybuf[1]
    out_ref[...] = _layer_norm(DEEPNORM_ALPHA * x1_ref[...] + y, g_ref[...], b_ref[...])


def _combine(plan, x1, meta_f, ys, ln_g, ln_b):
    t, d = x1.shape
    tm = min(ROW_TILE, t)
    row = lambda n: pl.BlockSpec((tm, n), lambda i: (i, 0))
    full = pl.BlockSpec((1, d), lambda i: (0, 0))
    return pl.pallas_call(
        functools.partial(_combine_kernel, tm=tm),
        grid=(t // tm,),
        in_specs=[_pos_spec(tm), row(d), row(LANES), pl.BlockSpec(memory_space=pl.ANY), full, full],
        out_specs=row(d),
        out_shape=jax.ShapeDtypeStruct((t, d), F32),
        scratch_shapes=[pltpu.VMEM((2, tm, d), F32), pltpu.SemaphoreType.DMA(())],
        compiler_params=_params(1),
        name="moe_combine",
    )(plan["pos"], x1, meta_f, ys, ln_g.reshape(1, d).astype(F32), ln_b.reshape(1, d).astype(F32))


def _routing_plan(meta_i, counts_row, n_tiles):
    counts = counts_row[0, MOE_GROUPS:MOE_GROUPS + N_EXPERTS].astype(I32)
    padded = ((counts + FFN_TILE - 1) // FFN_TILE) * FFN_TILE
    ends = jnp.cumsum(padded)
    offsets = ends - padded
    experts = jnp.arange(N_EXPERTS, dtype=I32)
    expert, rank = meta_i[0:2], meta_i[2:4]
    pos = rank + jnp.sum(jnp.where(expert[..., None] == experts, offsets, 0), axis=-1)
    tile_ends = ends // FFN_TILE
    tile_expert = jnp.sum(jnp.arange(n_tiles, dtype=I32)[:, None] >= tile_ends[None, :], axis=1)
    n_used = tile_ends[-1:].astype(I32)
    return dict(pos=pos.astype(I32), tile_expert=jnp.minimum(tile_expert, N_EXPERTS - 1).astype(I32),
                n_used=n_used, pad_base=(offsets + counts).astype(I32), pad_len=(padded - counts).astype(I32),
                tail=jnp.concatenate([n_used, jnp.full((1,), n_tiles, I32)]))


def _hier_moe_ln(x1, meta_i, meta_f, counts_row, w1, w3, w2, layer, ln_g, ln_b):
    t, d = x1.shape
    n_tiles = (2 * t) // FFN_TILE + N_EXPERTS
    plan = _routing_plan(meta_i, counts_row, n_tiles)
    xs = _dispatch(plan, x1, n_tiles * FFN_TILE)
    ys = _expert_ffn(plan, xs, w1, w3, w2, layer)
    return _combine(plan, x1, meta_f, ys, ln_g, ln_b)


def kernel(x, rel_table, da_wq, da_wk, da_wv, da_wo, da_lq1, da_lk1, da_lq2, da_lk2, da_subln_g, sb_wq, sb_wk,
           sb_wv, sb_wo, ln_mix_g, ln_mix_b, ln_ffn_g, ln_ffn_b, moe_w_group, moe_b_group, moe_w_expert,
           moe_b_expert, moe_w1, moe_w3, moe_w2):
    b, s, d = x.shape
    t = b * s
    assert s % DA_BLOCK == 0 and s % SB_BLOCK == 0 and t % min(ROW_TILE, t) == 0 and (2 * t) % FFN_TILE == 0
    xf = x.reshape(t, d).astype(F32)
    bias = _bias_tiles(rel_table, DA_BLOCK)
    for layer in range(DEPTH):
        j = layer // 2
        if layer % 2 == 0:
            q, k, v = _qkv_proj(xf, da_wq[j], da_wk[j], da_wv[j], DA_HEAD_DIM ** -0.5 * LOG2E)
            shape = (b, s, q.shape[1])
            lam_init = 0.8 - 0.6 * math.exp(-0.3 * layer)
            o = _diff_attention(q.reshape(shape), k.reshape(shape), v.reshape(shape), bias,
                                da_lq1[j], da_lk1[j], da_lq2[j], da_lk2[j], da_subln_g[j], lam_init)
            wo = da_wo[j]
        else:
            q, k, v = _qkv_proj(xf, sb_wq[j], sb_wk[j], sb_wv[j], -(SB_HEAD_DIM ** -0.5) * LOG2E)
            shape = (b, s, q.shape[1])
            o = _stick_breaking(q.reshape(shape), k.reshape(shape), v.reshape(shape))
            wo = sb_wo[j]
        x1, meta_i, meta_f, counts = _mix_ln_route(
            o.reshape(t, -1), xf, wo, ln_mix_g[layer], ln_mix_b[layer],
            moe_w_group[layer], moe_b_group[layer], moe_w_expert[layer], moe_b_expert[layer])
        xf = _hier_moe_ln(x1, meta_i, meta_f, counts, moe_w1, moe_w3, moe_w2, layer,
                          ln_ffn_g[layer], ln_ffn_b[layer])
    return xf.reshape(b, s, d).astype(x.dtype)
```

```python
import functools
import math

import jax
import jax.numpy as jnp
from jax import lax
from jax.experimental import pallas as pl
from jax.experimental.pallas import tpu as pltpu

F32 = jnp.float32
BF16 = jnp.bfloat16
I32 = jnp.int32

DEPTH = 2
DA_HEADS = 8
DA_HEAD_DIM = 64
SB_HEAD_DIM = 64
REL_BUCKETS = 32
REL_MAX_DIST = 128
MOE_GROUPS = 4
MOE_EXPERTS_PER_GROUP = 8
N_EXPERTS = MOE_GROUPS * MOE_EXPERTS_PER_GROUP
DEEPNORM_ALPHA = (2 * DEPTH) ** 0.25
LN_EPS = 1e-5
RMS_EPS = 1e-6
NEG_INF = -1e30
LOG2E = 1.4426950408889634
SB_DEAD_LOG2 = -160.0

LANES = 128
DA_BLOCK = 512
DA_PAR = 2
SB_BLOCK = 256
SB_PAR = 4
ROW_TILE = 512
FFN_TILE = 256
META_ROWS = 8
ROW_GROUP = 8
VMEM_LIMIT = 48 * 1024 * 1024


def _params(n_axes, vmem=VMEM_LIMIT):
    return pltpu.CompilerParams(dimension_semantics=("arbitrary",) * n_axes, vmem_limit_bytes=vmem)


def _proj_kernel(x_ref, wq_ref, wk_ref, wv_ref, q_ref, k_ref, v_ref, *, q_scale):
    x = x_ref[...].astype(BF16)
    q = jnp.dot(x, wq_ref[...], preferred_element_type=F32)
    q_ref[...] = (q * q_scale).astype(BF16)
    k_ref[...] = jnp.dot(x, wk_ref[...], preferred_element_type=F32).astype(BF16)
    v_ref[...] = jnp.dot(x, wv_ref[...], preferred_element_type=F32).astype(BF16)


def _qkv_proj(x2d, wq, wk, wv, q_scale):
    t, d = x2d.shape
    n = wq.shape[1]
    tm = min(ROW_TILE, t)
    row_in = pl.BlockSpec((tm, d), lambda i: (i, 0))
    w_spec = pl.BlockSpec((d, n), lambda i: (0, 0))
    row_out = pl.BlockSpec((tm, n), lambda i: (i, 0))
    return pl.pallas_call(
        functools.partial(_proj_kernel, q_scale=q_scale),
        grid=(t // tm,),
        in_specs=[row_in, w_spec, w_spec, w_spec],
        out_specs=[row_out, row_out, row_out],
        out_shape=[jax.ShapeDtypeStruct((t, n), BF16)] * 3,
        compiler_params=_params(1),
        name="qkv_proj",
    )(x2d, wq.astype(BF16), wk.astype(BF16), wv.astype(BF16))


def _bias_kernel(tab_ref, out_ref, *, blk):
    h = pl.program_id(0)
    i = lax.broadcasted_iota(I32, (blk, 2 * blk), 0)
    j = lax.broadcasted_iota(I32, (blk, 2 * blk), 1)
    n = blk + i - j
    nn = jnp.maximum(n, 0)
    max_exact = REL_BUCKETS // 2
    nf = jnp.maximum(nn, 1).astype(F32)
    large = max_exact + (jnp.log(nf * (1.0 / max_exact)) * ((REL_BUCKETS - max_exact) / math.log(REL_MAX_DIST / max_exact))
                         ).astype(I32)
    large = jnp.minimum(large, REL_BUCKETS - 1)
    bucket = jnp.where(nn < max_exact, nn, large)
    far = tab_ref[REL_BUCKETS - 1, h]
    acc = jnp.zeros((blk, 2 * blk), F32)
    for b in range(REL_BUCKETS):
        acc = jnp.where(bucket == b, tab_ref[b, h] - far, acc)
    out_ref[0] = jnp.where(n >= 0, acc * LOG2E, NEG_INF)


def _bias_tiles(rel_table, blk):
    heads = rel_table.shape[1]
    return pl.pallas_call(
        functools.partial(_bias_kernel, blk=blk),
        grid=(heads,),
        in_specs=[pl.BlockSpec(memory_space=pltpu.SMEM)],
        out_specs=pl.BlockSpec((1, blk, 2 * blk), lambda h: (h, 0, 0)),
        out_shape=jax.ShapeDtypeStruct((heads, blk, 2 * blk), F32),
        compiler_params=_params(1),
        name="rel_bias_tiles",
    )(rel_table.astype(F32))


def _split_streams(q):
    lane = lax.broadcasted_iota(I32, q.shape, 1)
    zero = jnp.zeros_like(q)
    return jnp.concatenate([jnp.where(lane < 64, q, zero), jnp.where(lane >= 64, q, zero)], axis=0)


def _da_kernel(q_ref, k_ref, v_ref, bias_ref, lq1_ref, lk1_ref, lq2_ref, lk2_ref, g_ref, o_ref,
               m_sc, l_sc, acc_sc, s_sc, *, blk, lam_init, n_par):
    qi = pl.program_id(2)
    heads = range(n_par)
    qs = [_split_streams(q_ref[0, :, c * LANES:(c + 1) * LANES]) for c in heads]
    m_sc[...] = jnp.full(m_sc.shape, NEG_INF, F32)
    l_sc[...] = jnp.zeros(l_sc.shape, F32)
    acc_sc[...] = jnp.zeros(acc_sc.shape, F32)

    def logits(ki):
        start = pl.multiple_of(ki * blk, blk)
        return [lax.dot_general(qs[c], k_ref[0, pl.ds(start, blk), c * LANES:(c + 1) * LANES],
                                (((1,), (1,)), ((), ())), preferred_element_type=F32) for c in heads]

    def consume(ki, bias_cols):
        start = pl.multiple_of(ki * blk, blk)
        alpha, p = [], []
        for c in heads:
            sc = s_sc[c]
            if bias_cols is not None:
                bias = bias_ref[c, :, bias_cols]
                sc = sc + jnp.concatenate([bias, bias], axis=0)
            m_prev = m_sc[c]
            m_next = jnp.maximum(m_prev, jnp.max(sc, axis=1, keepdims=True))
            m_sc[c] = m_next
            alpha.append(jnp.exp2(m_prev - m_next))
            pc = jnp.exp2(sc - jnp.concatenate([m_next] * (blk // LANES), axis=1))
            l_sc[c] = alpha[c] * l_sc[c] + sum(pc[:, n * LANES:(n + 1) * LANES] for n in range(blk // LANES))
            p.append(pc.astype(BF16))
        pv = [jnp.dot(p[c], v_ref[0, pl.ds(start, blk), c * LANES:(c + 1) * LANES],
                      preferred_element_type=F32) for c in heads]
        for c in heads:
            acc_sc[c] = alpha[c] * acc_sc[c] + pv[c]

    def step(ki, bias_cols):
        nxt = logits(ki + 1)
        consume(ki, bias_cols)
        for c in heads:
            s_sc[c] = nxt[c]

    first = logits(0)
    for c in heads:
        s_sc[c] = first[c]

    def far_body(ki, carry):
        step(ki, None)
        return carry

    lax.fori_loop(0, jnp.maximum(qi - 1, 0), far_body, 0)

    @pl.when(qi >= 1)
    def _():
        step(qi - 1, slice(0, blk))

    consume(qi, slice(blk, 2 * blk))

    lam = (jnp.exp(jnp.sum(lq1_ref[...] * lk1_ref[...], axis=1, keepdims=True))
           - jnp.exp(jnp.sum(lq2_ref[...] * lk2_ref[...], axis=1, keepdims=True)) + lam_init)
    for c in heads:
        inv_l = 1.0 / jnp.sum(l_sc[c], axis=1, keepdims=True)
        acc = acc_sc[c]
        o = acc[:blk] * inv_l[:blk] - lam * (acc[blk:] * inv_l[blk:])
        o = o * lax.rsqrt(jnp.mean(o * o, axis=1, keepdims=True) + RMS_EPS) * g_ref[...]
        o_ref[0, :, c * LANES:(c + 1) * LANES] = (o * (1.0 - lam_init)).astype(BF16)


def _diff_attention(q, k, v, bias, lq1, lk1, lq2, lk2, subln_g, lam_init):
    b, s, _ = q.shape
    blk, n_par = DA_BLOCK, DA_PAR
    q_spec = pl.BlockSpec((1, blk, n_par * LANES), lambda bi, h, qi: (bi, qi, h))
    kv_spec = pl.BlockSpec((1, s, n_par * LANES), lambda bi, h, qi: (bi, 0, h))
    vec64 = pl.BlockSpec((1, DA_HEAD_DIM), lambda bi, h, qi: (0, 0))
    state = pltpu.VMEM((n_par, 2 * blk, LANES), F32)
    return pl.pallas_call(
        functools.partial(_da_kernel, blk=blk, lam_init=lam_init, n_par=n_par),
        grid=(b, DA_HEADS // n_par, s // blk),
        in_specs=[q_spec, kv_spec, kv_spec,
                  pl.BlockSpec((n_par, blk, 2 * blk), lambda bi, h, qi: (h, 0, 0)),
                  vec64, vec64, vec64, vec64,
                  pl.BlockSpec((1, LANES), lambda bi, h, qi: (0, 0))],
        out_specs=q_spec,
        out_shape=jax.ShapeDtypeStruct(q.shape, BF16),
        scratch_shapes=[state, state, state, pltpu.VMEM((n_par, 2 * blk, blk), F32)],
        compiler_params=_params(3),
        name="diff_attention",
    )(q, k, v, bias, lq1.reshape(1, -1), lk1.reshape(1, -1), lq2.reshape(1, -1), lk2.reshape(1, -1),
      subln_g.reshape(1, -1))


def _sb_kernel(q_ref, k_ref, v_ref, o_ref, r_sc, acc_sc, *, blk, n_par):
    qi = pl.program_id(2)
    rows = 2 * blk
    qs = [_split_streams(q_ref[0, :, c * LANES:(c + 1) * LANES]) for c in range(n_par)]
    jj = lax.broadcasted_iota(I32, (blk, blk), 0)
    ss = lax.broadcasted_iota(I32, (blk, blk), 1)
    later = jnp.where(jj > ss, 1.0, 0.0).astype(BF16)
    lane = lax.broadcasted_iota(I32, (blk, LANES), 1)
    r_sc[...] = jnp.zeros(r_sc.shape, F32)
    acc_sc[...] = jnp.zeros(acc_sc.shape, F32)

    def blocks(ki, diagonal):
        start = pl.multiple_of(ki * blk, blk)
        pairs = range(n_par)
        zn = [lax.dot_general(qs[c], k_ref[0, pl.ds(start, blk), c * LANES:(c + 1) * LANES],
                              (((1,), (1,)), ((), ())), preferred_element_type=F32) for c in pairs]
        if diagonal:
            r = lax.broadcasted_iota(I32, (rows, blk), 0)
            t = jnp.where(r >= blk, r - blk, r)
            strict = lax.broadcasted_iota(I32, (rows, blk), 1) < t
        log_fail, log_beta, hi, lo = [], [], [], []
        for c in pairs:
            softplus = jnp.log(1.0 + jnp.exp2(-jnp.abs(zn[c]))) * LOG2E
            lf = jnp.minimum(zn[c], 0.0) - softplus
            log_beta.append(lf - zn[c])
            if diagonal:
                lf = jnp.where(strict, lf, 0.0)
            log_fail.append(lf)
            hi.append(lf.astype(BF16))
            lo.append((lf - hi[c].astype(F32)).astype(BF16))
        suffix = [jnp.dot(hi[c], later, preferred_element_type=F32)
                  + jnp.dot(lo[c], later, preferred_element_type=F32) for c in pairs]
        wb = []
        for c in pairs:
            w = jnp.exp2(log_beta[c] + suffix[c])
            if diagonal:
                w = jnp.where(strict, w, 0.0)
            wb.append(w.astype(BF16))
        pv = []
        for c in pairs:
            vb = v_ref[0, pl.ds(start, blk), c * LANES:(c + 1) * LANES]
            pv.append((jnp.dot(wb[c][:blk], vb, preferred_element_type=F32),
                       jnp.dot(wb[c][blk:], vb, preferred_element_type=F32)))
        for c in pairs:
            carry = jnp.exp2(r_sc[c])
            acc_sc[c] += jnp.where(lane < 64, carry[:blk] * pv[c][0], carry[blk:] * pv[c][1])
            r_sc[c] += suffix[c][:, 0:1] + log_fail[c][:, 0:1]

    blocks(qi, True)

    def alive():
        return jnp.max(r_sc[...]) > SB_DEAD_LOG2

    def cond(carry):
        step, live = carry
        return jnp.logical_and(step < qi, live)

    def body(carry):
        step, _ = carry
        blocks(qi - 1 - step, False)
        return step + 1, alive()

    lax.while_loop(cond, body, (jnp.int32(0), alive()))
    for c in range(n_par):
        o_ref[0, :, c * LANES:(c + 1) * LANES] = acc_sc[c].astype(BF16)


def _stick_breaking(q, k, v):
    b, s, width = q.shape
    blk, n_par = SB_BLOCK, SB_PAR
    q_spec = pl.BlockSpec((1, blk, n_par * LANES), lambda bi, h, qi: (bi, qi, h))
    kv_spec = pl.BlockSpec((1, s, n_par * LANES), lambda bi, h, qi: (bi, 0, h))
    return pl.pallas_call(
        functools.partial(_sb_kernel, blk=blk, n_par=n_par),
        grid=(b, width // (n_par * LANES), s // blk),
        in_specs=[q_spec, kv_spec, kv_spec],
        out_specs=q_spec,
        out_shape=jax.ShapeDtypeStruct(q.shape, BF16),
        scratch_shapes=[pltpu.VMEM((n_par, 2 * blk, LANES), F32), pltpu.VMEM((n_par, blk, LANES), F32)],
        compiler_params=_params(3),
        name="stick_breaking",
    )(q, k, v)


def _layer_norm(y, g, b):
    mu = jnp.mean(y, axis=1, keepdims=True)
    yc = y - mu
    var = jnp.mean(yc * yc, axis=1, keepdims=True)
    return yc * lax.rsqrt(var + LN_EPS) * g + b


def _first_index_of_max(vals, lane_f):
    m = jnp.max(vals, axis=1, keepdims=True)
    idx = jnp.min(jnp.where(vals == m, lane_f, float(LANES)), axis=1, keepdims=True)
    return m, idx


def _mix_ln_route_kernel(o_ref, x_ref, wo_ref, g_ref, b_ref, wr_hi_ref, wr_lo_ref, br_ref,
                         x1_ref, x1t_ref, mi_ref, mf_ref, cnt_ref, *, tm):
    step = pl.program_id(0)

    @pl.when(step == 0)
    def _():
        cnt_ref[...] = jnp.zeros(cnt_ref.shape, F32)

    mix = jnp.dot(o_ref[...], wo_ref[...], preferred_element_type=F32)
    x1 = _layer_norm(DEEPNORM_ALPHA * x_ref[...] + mix, g_ref[...], b_ref[...])
    x1_ref[...] = x1
    _store_token_tiles(x1t_ref, x1)

    xh = x1.astype(BF16)
    xl = (x1 - xh.astype(F32)).astype(BF16)
    logits = (jnp.dot(xh, wr_hi_ref[...], preferred_element_type=F32)
              + jnp.dot(xh, wr_lo_ref[...], preferred_element_type=F32)
              + jnp.dot(xl, wr_hi_ref[...], preferred_element_type=F32)) + br_ref[...]
    lane = lax.broadcasted_iota(I32, (tm, LANES), 1)
    lane_f = lane.astype(F32)
    is_group = lane < MOE_GROUPS
    g_max, g_idx = _first_index_of_max(jnp.where(is_group, logits, NEG_INF), lane_f)
    g_gate = 1.0 / jnp.sum(jnp.where(is_group, jnp.exp(logits - g_max), 0.0), axis=1, keepdims=True)
    first = MOE_GROUPS + MOE_EXPERTS_PER_GROUP * g_idx
    in_group = (lane_f >= first) & (lane_f < first + MOE_EXPERTS_PER_GROUP)
    e_logits = jnp.where(in_group, logits, NEG_INF)
    m1, i1 = _first_index_of_max(e_logits, lane_f)
    m2, i2 = _first_index_of_max(jnp.where(lane_f == i1, NEG_INF, e_logits), lane_f)
    ratio = jnp.exp(m2 - m1)
    w1 = g_gate / (1.0 + ratio)
    w2 = w1 * ratio

    e1 = jnp.where(lane_f == i1, 1.0, 0.0)
    e2 = jnp.where(lane_f == i2, 1.0, 0.0)
    both = e1 + e2
    rr = lax.broadcasted_iota(I32, (tm, tm), 0)
    cc = lax.broadcasted_iota(I32, (tm, tm), 1)
    earlier = jnp.where(rr > cc, 1.0, 0.0).astype(BF16)
    before = cnt_ref[...] + jnp.dot(earlier, both.astype(BF16), preferred_element_type=F32)
    rank1 = jnp.sum(e1 * before, axis=1, keepdims=True)
    rank2 = jnp.sum(e2 * before, axis=1, keepdims=True)
    cnt_ref[...] += jnp.sum(both, axis=0, keepdims=True)

    meta = jnp.where(lane == 0, i1 - MOE_GROUPS,
                     jnp.where(lane == 1, i2 - MOE_GROUPS,
                               jnp.where(lane == 2, rank1, jnp.where(lane == 3, rank2, 0.0))))
    mi_ref[...] = meta.T[:META_ROWS].astype(I32)
    mf_ref[...] = jnp.where(lane == 0, w1, jnp.where(lane == 1, w2, 0.0))


def _mix_ln_route(o2d, x2d, wo, ln_g, ln_b, w_group, b_group, w_expert, b_expert):
    t, d = x2d.shape
    tm = min(ROW_TILE, t)
    pad = LANES - MOE_GROUPS - N_EXPERTS
    wr = jnp.pad(jnp.concatenate([w_group, w_expert], axis=1).astype(F32), ((0, 0), (0, pad)))
    wr_hi = wr.astype(BF16)
    wr_lo = (wr - wr_hi.astype(F32)).astype(BF16)
    br = jnp.pad(jnp.concatenate([b_group, b_expert]).astype(F32), (0, pad)).reshape(1, LANES)
    row = lambda n: pl.BlockSpec((tm, n), lambda i: (i, 0))
    full = lambda r, c: pl.BlockSpec((r, c), lambda i: (0, 0))
    return pl.pallas_call(
        functools.partial(_mix_ln_route_kernel, tm=tm),
        grid=(t // tm,),
        in_specs=[row(o2d.shape[1]), row(d), full(o2d.shape[1], d), full(1, d), full(1, d),
                  full(d, LANES), full(d, LANES), full(1, LANES)],
        out_specs=[row(d), pl.BlockSpec((tm * ROW_GROUP, LANES), lambda i: (i, 0)),
                   pl.BlockSpec((META_ROWS, tm), lambda i: (0, i)), row(LANES), full(1, LANES)],
        out_shape=[jax.ShapeDtypeStruct((t, d), F32), jax.ShapeDtypeStruct((t * ROW_GROUP, LANES), F32),
                   jax.ShapeDtypeStruct((META_ROWS, t), I32),
                   jax.ShapeDtypeStruct((t, LANES), F32), jax.ShapeDtypeStruct((1, LANES), F32)],
        compiler_params=_params(1),
        name="mix_ln_route",
    )(o2d, x2d, wo.astype(BF16), ln_g.reshape(1, d).astype(F32), ln_b.reshape(1, d).astype(F32),
      wr_hi, wr_lo, br)


def _load_token_tiles(ref, rows):
    return jnp.concatenate([ref[pl.ds(s, rows, stride=ROW_GROUP), :] for s in range(ROW_GROUP)], axis=1)


def _store_token_tiles(ref, val):
    for s in range(ROW_GROUP):
        ref[pl.ds(s, val.shape[0], stride=ROW_GROUP), :] = val[:, s * LANES:(s + 1) * LANES]


def _start_row_copies(tm, copy_of):
    def group(g, carry):
        for r in range(ROW_GROUP):
            for slot in range(2):
                copy_of(g * ROW_GROUP + r, slot).start(priority=slot)
        return carry

    lax.fori_loop(0, tm // ROW_GROUP, group, 0)


def _wait_row_copies(tm, copy_of):
    def group(g, carry):
        for _ in range(2 * ROW_GROUP):
            copy_of(0, 0).wait()
        return carry

    lax.fori_loop(0, tm // ROW_GROUP, group, 0)


def _row(ref, row):
    return ref.at[pl.ds(pl.multiple_of(row * ROW_GROUP, ROW_GROUP), ROW_GROUP)]


def _dispatch_kernel(pad_base_ref, pad_len_ref, tail_ref, pos_ref, x_ref, xs_ref, zeros_sc, sem, *, tm, tile):
    @pl.when(pl.program_id(0) == 0)
    def _():
        zeros_sc[...] = jnp.zeros(zeros_sc.shape, F32)

        def pad_rows(e, carry):
            def one(r, c, wait):
                cp = pltpu.make_async_copy(_row(zeros_sc, 0), _row(xs_ref, pad_base_ref[e] + r), sem)
                cp.wait() if wait else cp.start()
                return c
            lax.fori_loop(0, pad_len_ref[e], functools.partial(one, wait=False), 0)
            lax.fori_loop(0, pad_len_ref[e], functools.partial(one, wait=True), 0)
            return carry

        lax.fori_loop(0, N_EXPERTS, pad_rows, 0)

        def tail_tile(i, carry):
            n = tile * ROW_GROUP
            cp = pltpu.make_async_copy(zeros_sc, xs_ref.at[pl.ds(pl.multiple_of(i * n, n), n)], sem)
            cp.start()
            cp.wait()
            return carry

        lax.fori_loop(tail_ref[0], tail_ref[1], tail_tile, 0)

    copy_of = lambda row, slot: pltpu.make_async_copy(_row(x_ref, row), _row(xs_ref, pos_ref[slot * tm + row]), sem)
    _start_row_copies(tm, copy_of)
    _wait_row_copies(tm, copy_of)


def _pos_spec(tm):
    return pl.BlockSpec((2 * tm,), lambda i, *prefetch: (i,), memory_space=pltpu.SMEM)


def _token_tile_spec(rows, index_map):
    return pl.BlockSpec((rows * ROW_GROUP, LANES), index_map)


def _dispatch(plan, x1t, n_rows):
    t = x1t.shape[0] // ROW_GROUP
    tm = min(ROW_TILE, t)
    grid_spec = pltpu.PrefetchScalarGridSpec(
        num_scalar_prefetch=3,
        grid=(t // tm,),
        in_specs=[_pos_spec(tm), _token_tile_spec(tm, lambda i, *prefetch: (i, 0))],
        out_specs=pl.BlockSpec(memory_space=pl.ANY),
        scratch_shapes=[pltpu.VMEM((FFN_TILE * ROW_GROUP, LANES), F32), pltpu.SemaphoreType.DMA(())],
    )
    return pl.pallas_call(
        functools.partial(_dispatch_kernel, tm=tm, tile=FFN_TILE),
        grid_spec=grid_spec,
        out_shape=jax.ShapeDtypeStruct((n_rows * ROW_GROUP, LANES), F32),
        compiler_params=_params(1),
        name="moe_dispatch",
    )(plan["pad_base"], plan["pad_len"], plan["tail"], plan["pos"], x1t)


def _ffn_kernel(tile_expert_ref, n_used_ref, xs_ref, w1_ref, w3_ref, w2_ref, ys_ref, w1_sc, w3_sc, w2_sc):
    i = pl.program_id(0)
    used = i < n_used_ref[0]
    new_expert = jnp.logical_or(i == 0, tile_expert_ref[i] != tile_expert_ref[jnp.maximum(i - 1, 0)])

    @pl.when(jnp.logical_and(used, new_expert))
    def _():
        w1_sc[...] = w1_ref[0, 0].astype(BF16)
        w3_sc[...] = w3_ref[0, 0].astype(BF16)
        w2_sc[...] = w2_ref[0, 0].astype(BF16)

    @pl.when(used)
    def _():
        x = _load_token_tiles(xs_ref, xs_ref.shape[0] // ROW_GROUP).astype(BF16)
        a = jnp.dot(x, w1_sc[...], preferred_element_type=F32)
        b = jnp.dot(x, w3_sc[...], preferred_element_type=F32)
        h = (a * jax.nn.sigmoid(a) * b).astype(BF16)
        _store_token_tiles(ys_ref, jnp.dot(h, w2_sc[...], preferred_element_type=F32))

    @pl.when(jnp.logical_not(used))
    def _():
        ys_ref[...] = jnp.zeros(ys_ref.shape, F32)


def _expert_ffn(plan, xs, w1, w3, w2, layer):
    n_rows = xs.shape[0] // ROW_GROUP
    d, hidden = w1.shape[2], w1.shape[3]
    assert d == ROW_GROUP * LANES
    tm = FFN_TILE
    x_spec = _token_tile_spec(tm, lambda i, te, nu: (jnp.minimum(i, nu[0] - 1), 0))
    w_in = pl.BlockSpec((1, 1, d, hidden), lambda i, te, nu: (layer, te[i], 0, 0))
    w_out = pl.BlockSpec((1, 1, hidden, d), lambda i, te, nu: (layer, te[i], 0, 0))
    grid_spec = pltpu.PrefetchScalarGridSpec(
        num_scalar_prefetch=2,
        grid=(n_rows // tm,),
        in_specs=[x_spec, w_in, w_in, w_out],
        out_specs=_token_tile_spec(tm, lambda i, te, nu: (i, 0)),
        scratch_shapes=[pltpu.VMEM((d, hidden), BF16), pltpu.VMEM((d, hidden), BF16),
                        pltpu.VMEM((hidden, d), BF16)],
    )
    return pl.pallas_call(
        _ffn_kernel,
        grid_spec=grid_spec,
        out_shape=jax.ShapeDtypeStruct(xs.shape, F32),
        compiler_params=_params(1),
        name="moe_experts",
    )(plan["tile_expert"], plan["n_used"], xs, w1.astype(F32), w3.astype(F32), w2.astype(F32))


def _combine_kernel(pos_ref, x1_ref, mf_ref, ys_ref, g_ref, b_ref, out_ref, ybuf, sem, *, tm):
    copy_of = lambda row, slot: pltpu.make_async_copy(_row(ys_ref, pos_ref[slot * tm + row]),
                                                      _row(ybuf.at[slot], row), sem)
    _start_row_copies(tm, copy_of)
    _wait_row_copies(tm, copy_of)
    gates = mf_ref[...]
    y = gates[:, 0:1] * _load_token_tiles(ybuf.at[0], tm) + gates[:, 1:2] * _load_token_tiles(ybuf.at[1], tm)
    out_ref[...] = _layer_norm(DEEPNORM_ALPHA * x1_ref[...] + y, g_ref[...], b_ref[...])


def _combine(plan, x1, meta_f, ys, ln_g, ln_b):
    t, d = x1.shape
    tm = min(ROW_TILE, t)
    row = lambda n: pl.BlockSpec((tm, n), lambda i: (i, 0))
    full = pl.BlockSpec((1, d), lambda i: (0, 0))
    return pl.pallas_call(
        functools.partial(_combine_kernel, tm=tm),
        grid=(t // tm,),
        in_specs=[_pos_spec(tm), row(d), row(LANES), pl.BlockSpec(memory_space=pl.ANY), full, full],
        out_specs=row(d),
        out_shape=jax.ShapeDtypeStruct((t, d), F32),
        scratch_shapes=[pltpu.VMEM((2, tm * ROW_GROUP, LANES), F32), pltpu.SemaphoreType.DMA(())],
        compiler_params=_params(1),
        name="moe_combine",
    )(plan["pos"], x1, meta_f, ys, ln_g.reshape(1, d).astype(F32), ln_b.reshape(1, d).astype(F32))


def _routing_plan(meta_i, counts_row, n_tiles):
    counts = counts_row[0, MOE_GROUPS:MOE_GROUPS + N_EXPERTS].astype(I32)
    padded = ((counts + FFN_TILE - 1) // FFN_TILE) * FFN_TILE
    ends = jnp.cumsum(padded)
    offsets = ends - padded
    experts = jnp.arange(N_EXPERTS, dtype=I32)
    expert, rank = meta_i[0:2], meta_i[2:4]
    pos = rank + jnp.sum(jnp.where(expert[..., None] == experts, offsets, 0), axis=-1)
    tile_ends = ends // FFN_TILE
    tile_expert = jnp.sum(jnp.arange(n_tiles, dtype=I32)[:, None] >= tile_ends[None, :], axis=1)
    n_used = tile_ends[-1:].astype(I32)
    tm = min(ROW_TILE, pos.shape[1])
    pos = pos.astype(I32).reshape(2, -1, tm).transpose(1, 0, 2).reshape(-1)
    return dict(pos=pos, tile_expert=jnp.minimum(tile_expert, N_EXPERTS - 1).astype(I32),
                n_used=n_used, pad_base=(offsets + counts).astype(I32), pad_len=(padded - counts).astype(I32),
                tail=jnp.concatenate([n_used, jnp.full((1,), n_tiles, I32)]))


def _hier_moe_ln(x1, x1t, meta_i, meta_f, counts_row, w1, w3, w2, layer, ln_g, ln_b):
    t, d = x1.shape
    n_tiles = (2 * t) // FFN_TILE + N_EXPERTS
    plan = _routing_plan(meta_i, counts_row, n_tiles)
    xs = _dispatch(plan, x1t, n_tiles * FFN_TILE)
    ys = _expert_ffn(plan, xs, w1, w3, w2, layer)
    return _combine(plan, x1, meta_f, ys, ln_g, ln_b)


def kernel(x, rel_table, da_wq, da_wk, da_wv, da_wo, da_lq1, da_lk1, da_lq2, da_lk2, da_subln_g, sb_wq, sb_wk,
           sb_wv, sb_wo, ln_mix_g, ln_mix_b, ln_ffn_g, ln_ffn_b, moe_w_group, moe_b_group, moe_w_expert,
           moe_b_expert, moe_w1, moe_w3, moe_w2):
    b, s, d = x.shape
    t = b * s
    assert s % DA_BLOCK == 0 and s % SB_BLOCK == 0 and t % min(ROW_TILE, t) == 0 and (2 * t) % FFN_TILE == 0
    xf = x.reshape(t, d).astype(F32)
    bias = _bias_tiles(rel_table, DA_BLOCK)
    for layer in range(DEPTH):
        j = layer // 2
        if layer % 2 == 0:
            q, k, v = _qkv_proj(xf, da_wq[j], da_wk[j], da_wv[j], DA_HEAD_DIM ** -0.5 * LOG2E)
            shape = (b, s, q.shape[1])
            lam_init = 0.8 - 0.6 * math.exp(-0.3 * layer)
            o = _diff_attention(q.reshape(shape), k.reshape(shape), v.reshape(shape), bias,
                                da_lq1[j], da_lk1[j], da_lq2[j], da_lk2[j], da_subln_g[j], lam_init)
            wo = da_wo[j]
        else:
            q, k, v = _qkv_proj(xf, sb_wq[j], sb_wk[j], sb_wv[j], -(SB_HEAD_DIM ** -0.5) * LOG2E)
            shape = (b, s, q.shape[1])
            o = _stick_breaking(q.reshape(shape), k.reshape(shape), v.reshape(shape))
            wo = sb_wo[j]
        x1, x1t, meta_i, meta_f, counts = _mix_ln_route(
            o.reshape(t, -1), xf, wo, ln_mix_g[layer], ln_mix_b[layer],
            moe_w_group[layer], moe_b_group[layer], moe_w_expert[layer], moe_b_expert[layer])
        xf = _hier_moe_ln(x1, x1t, meta_i, meta_f, counts, moe_w1, moe_w3, moe_w2, layer,
                          ln_ffn_g[layer], ln_ffn_b[layer])
    return xf.reshape(b, s, d).astype(x.dtype)
```

```python
import functools
import math

import jax
import jax.numpy as jnp
from jax import lax
from jax.experimental import pallas as pl
from jax.experimental.pallas import tpu as pltpu

F32 = jnp.float32
BF16 = jnp.bfloat16
I32 = jnp.int32

DEPTH = 2
DA_HEADS = 8
DA_HEAD_DIM = 64
SB_HEAD_DIM = 64
REL_BUCKETS = 32
REL_MAX_DIST = 128
MOE_GROUPS = 4
MOE_EXPERTS_PER_GROUP = 8
N_EXPERTS = MOE_GROUPS * MOE_EXPERTS_PER_GROUP
DEEPNORM_ALPHA = (2 * DEPTH) ** 0.25
LN_EPS = 1e-5
RMS_EPS = 1e-6
NEG_INF = -1e30
LOG2E = 1.4426950408889634
SB_DEAD_LOG2 = -160.0

LANES = 128
DA_BLOCK = 512
DA_PAR = 2
SB_BLOCK = 256
SB_PAR = 4
ROW_TILE = 512
FFN_TILE = 256
META_ROWS = 8
ROW_GROUP = 8
VMEM_LIMIT = 48 * 1024 * 1024


def _params(n_axes, vmem=VMEM_LIMIT):
    return pltpu.CompilerParams(dimension_semantics=("arbitrary",) * n_axes, vmem_limit_bytes=vmem)


def _proj_kernel(x_ref, wq_ref, wk_ref, wv_ref, q_ref, k_ref, v_ref, *, q_scale):
    x = x_ref[...].astype(BF16)
    q = jnp.dot(x, wq_ref[...], preferred_element_type=F32)
    q_ref[...] = (q * q_scale).astype(BF16)
    k_ref[...] = jnp.dot(x, wk_ref[...], preferred_element_type=F32).astype(BF16)
    v_ref[...] = jnp.dot(x, wv_ref[...], preferred_element_type=F32).astype(BF16)


def _qkv_proj(x2d, wq, wk, wv, q_scale):
    t, d = x2d.shape
    n = wq.shape[1]
    tm = min(ROW_TILE, t)
    row_in = pl.BlockSpec((tm, d), lambda i: (i, 0))
    w_spec = pl.BlockSpec((d, n), lambda i: (0, 0))
    row_out = pl.BlockSpec((tm, n), lambda i: (i, 0))
    return pl.pallas_call(
        functools.partial(_proj_kernel, q_scale=q_scale),
        grid=(t // tm,),
        in_specs=[row_in, w_spec, w_spec, w_spec],
        out_specs=[row_out, row_out, row_out],
        out_shape=[jax.ShapeDtypeStruct((t, n), BF16)] * 3,
        compiler_params=_params(1),
        name="qkv_proj",
    )(x2d, wq.astype(BF16), wk.astype(BF16), wv.astype(BF16))


def _bias_kernel(tab_ref, out_ref, *, blk):
    h = pl.program_id(0)
    i = lax.broadcasted_iota(I32, (blk, 2 * blk), 0)
    j = lax.broadcasted_iota(I32, (blk, 2 * blk), 1)
    n = blk + i - j
    nn = jnp.maximum(n, 0)
    max_exact = REL_BUCKETS // 2
    nf = jnp.maximum(nn, 1).astype(F32)
    large = max_exact + (jnp.log(nf * (1.0 / max_exact)) * ((REL_BUCKETS - max_exact) / math.log(REL_MAX_DIST / max_exact))
                         ).astype(I32)
    large = jnp.minimum(large, REL_BUCKETS - 1)
    bucket = jnp.where(nn < max_exact, nn, large)
    far = tab_ref[REL_BUCKETS - 1, h]
    acc = jnp.zeros((blk, 2 * blk), F32)
    for b in range(REL_BUCKETS):
        acc = jnp.where(bucket == b, tab_ref[b, h] - far, acc)
    out_ref[0] = jnp.where(n >= 0, acc * LOG2E, NEG_INF)


def _bias_tiles(rel_table, blk):
    heads = rel_table.shape[1]
    return pl.pallas_call(
        functools.partial(_bias_kernel, blk=blk),
        grid=(heads,),
        in_specs=[pl.BlockSpec(memory_space=pltpu.SMEM)],
        out_specs=pl.BlockSpec((1, blk, 2 * blk), lambda h: (h, 0, 0)),
        out_shape=jax.ShapeDtypeStruct((heads, blk, 2 * blk), F32),
        compiler_params=_params(1),
        name="rel_bias_tiles",
    )(rel_table.astype(F32))


def _split_streams(q):
    lane = lax.broadcasted_iota(I32, q.shape, 1)
    zero = jnp.zeros_like(q)
    return jnp.concatenate([jnp.where(lane < 64, q, zero), jnp.where(lane >= 64, q, zero)], axis=0)


def _da_kernel(q_ref, k_ref, v_ref, bias_ref, lq1_ref, lk1_ref, lq2_ref, lk2_ref, g_ref, o_ref,
               m_sc, l_sc, acc_sc, s_sc, *, blk, lam_init, n_par):
    qi = pl.program_id(2)
    heads = range(n_par)
    qs = [_split_streams(q_ref[0, :, c * LANES:(c + 1) * LANES]) for c in heads]
    m_sc[...] = jnp.full(m_sc.shape, NEG_INF, F32)
    l_sc[...] = jnp.zeros(l_sc.shape, F32)
    acc_sc[...] = jnp.zeros(acc_sc.shape, F32)

    def logits(ki):
        start = pl.multiple_of(ki * blk, blk)
        return [lax.dot_general(qs[c], k_ref[0, pl.ds(start, blk), c * LANES:(c + 1) * LANES],
                                (((1,), (1,)), ((), ())), preferred_element_type=F32) for c in heads]

    def consume(ki, bias_cols):
        start = pl.multiple_of(ki * blk, blk)
        alpha, p = [], []
        for c in heads:
            sc = s_sc[c]
            if bias_cols is not None:
                bias = bias_ref[c, :, bias_cols]
                sc = sc + jnp.concatenate([bias, bias], axis=0)
            m_prev = m_sc[c]
            m_next = jnp.maximum(m_prev, jnp.max(sc, axis=1, keepdims=True))
            m_sc[c] = m_next
            alpha.append(jnp.exp2(m_prev - m_next))
            pc = jnp.exp2(sc - jnp.concatenate([m_next] * (blk // LANES), axis=1))
            l_sc[c] = alpha[c] * l_sc[c] + sum(pc[:, n * LANES:(n + 1) * LANES] for n in range(blk // LANES))
            p.append(pc.astype(BF16))
        pv = [jnp.dot(p[c], v_ref[0, pl.ds(start, blk), c * LANES:(c + 1) * LANES],
                      preferred_element_type=F32) for c in heads]
        for c in heads:
            acc_sc[c] = alpha[c] * acc_sc[c] + pv[c]

    def step(ki, bias_cols):
        nxt = logits(ki + 1)
        consume(ki, bias_cols)
        for c in heads:
            s_sc[c] = nxt[c]

    first = logits(0)
    for c in heads:
        s_sc[c] = first[c]

    def far_body(ki, carry):
        step(ki, None)
        return carry

    lax.fori_loop(0, jnp.maximum(qi - 1, 0), far_body, 0)

    @pl.when(qi >= 1)
    def _():
        step(qi - 1, slice(0, blk))

    consume(qi, slice(blk, 2 * blk))

    lam = (jnp.exp(jnp.sum(lq1_ref[...] * lk1_ref[...], axis=1, keepdims=True))
           - jnp.exp(jnp.sum(lq2_ref[...] * lk2_ref[...], axis=1, keepdims=True)) + lam_init)
    for c in heads:
        inv_l = 1.0 / jnp.sum(l_sc[c], axis=1, keepdims=True)
        acc = acc_sc[c]
        o = acc[:blk] * inv_l[:blk] - lam * (acc[blk:] * inv_l[blk:])
        o = o * lax.rsqrt(jnp.mean(o * o, axis=1, keepdims=True) + RMS_EPS) * g_ref[...]
        o_ref[0, :, c * LANES:(c + 1) * LANES] = (o * (1.0 - lam_init)).astype(BF16)


def _diff_attention(q, k, v, bias, lq1, lk1, lq2, lk2, subln_g, lam_init):
    b, s, _ = q.shape
    blk, n_par = DA_BLOCK, DA_PAR
    q_spec = pl.BlockSpec((1, blk, n_par * LANES), lambda bi, h, qi: (bi, qi, h))
    kv_spec = pl.BlockSpec((1, s, n_par * LANES), lambda bi, h, qi: (bi, 0, h))
    vec64 = pl.BlockSpec((1, DA_HEAD_DIM), lambda bi, h, qi: (0, 0))
    state = pltpu.VMEM((n_par, 2 * blk, LANES), F32)
    return pl.pallas_call(
        functools.partial(_da_kernel, blk=blk, lam_init=lam_init, n_par=n_par),
        grid=(b, DA_HEADS // n_par, s // blk),
        in_specs=[q_spec, kv_spec, kv_spec,
                  pl.BlockSpec((n_par, blk, 2 * blk), lambda bi, h, qi: (h, 0, 0)),
                  vec64, vec64, vec64, vec64,
                  pl.BlockSpec((1, LANES), lambda bi, h, qi: (0, 0))],
        out_specs=q_spec,
        out_shape=jax.ShapeDtypeStruct(q.shape, BF16),
        scratch_shapes=[state, state, state, pltpu.VMEM((n_par, 2 * blk, blk), F32)],
        compiler_params=_params(3),
        name="diff_attention",
    )(q, k, v, bias, lq1.reshape(1, -1), lk1.reshape(1, -1), lq2.reshape(1, -1), lk2.reshape(1, -1),
      subln_g.reshape(1, -1))


def _sb_kernel(q_ref, k_ref, v_ref, o_ref, r_sc, acc_sc, *, blk, n_par):
    qi = pl.program_id(2)
    rows = 2 * blk
    qs = [_split_streams(q_ref[0, :, c * LANES:(c + 1) * LANES]) for c in range(n_par)]
    jj = lax.broadcasted_iota(I32, (blk, blk), 0)
    ss = lax.broadcasted_iota(I32, (blk, blk), 1)
    later = jnp.where(jj > ss, 1.0, 0.0).astype(BF16)
    later2 = jnp.concatenate([later, later], axis=0)
    lane = lax.broadcasted_iota(I32, (blk, LANES), 1)
    r_sc[...] = jnp.zeros(r_sc.shape, F32)
    acc_sc[...] = jnp.zeros(acc_sc.shape, F32)

    def blocks(ki, diagonal):
        start = pl.multiple_of(ki * blk, blk)
        pairs = range(n_par)
        zn = [lax.dot_general(qs[c], k_ref[0, pl.ds(start, blk), c * LANES:(c + 1) * LANES],
                              (((1,), (1,)), ((), ())), preferred_element_type=F32) for c in pairs]
        if diagonal:
            r = lax.broadcasted_iota(I32, (rows, blk), 0)
            t = jnp.where(r >= blk, r - blk, r)
            strict = lax.broadcasted_iota(I32, (rows, blk), 1) < t
        log_fail, log_beta, hi_lo = [], [], []
        for c in pairs:
            neg_abs = pltpu.bitcast(pltpu.bitcast(zn[c], jnp.uint32) | jnp.uint32(0x80000000), F32)
            softplus = jnp.log(1.0 + jnp.exp2(neg_abs)) * LOG2E
            lf = jnp.minimum(zn[c], 0.0) - softplus
            log_beta.append(lf - zn[c])
            if diagonal:
                lf = jnp.where(strict, lf, 0.0)
            log_fail.append(lf)
            hi = pltpu.bitcast(pltpu.bitcast(lf, jnp.uint32) & jnp.uint32(0xFFFF0000), F32)
            hi_lo.append(jnp.concatenate([hi.astype(BF16), (lf - hi).astype(BF16)], axis=1))
        suffix = [jnp.dot(hi_lo[c], later2, preferred_element_type=F32) for c in pairs]
        wb = []
        for c in pairs:
            w = jnp.exp2(log_beta[c] + suffix[c])
            if diagonal:
                w = jnp.where(strict, w, 0.0)
            wb.append(w.astype(BF16))
        pv = []
        for c in pairs:
            vb = v_ref[0, pl.ds(start, blk), c * LANES:(c + 1) * LANES]
            pv.append((jnp.dot(wb[c][:blk], vb, preferred_element_type=F32),
                       jnp.dot(wb[c][blk:], vb, preferred_element_type=F32)))
        for c in pairs:
            carry = jnp.exp2(r_sc[c])
            acc_sc[c] += jnp.where(lane < 64, carry[:blk] * pv[c][0], carry[blk:] * pv[c][1])
            r_sc[c] += suffix[c][:, 0:1] + log_fail[c][:, 0:1]

    blocks(qi, True)

    def alive():
        return jnp.max(r_sc[...]) > SB_DEAD_LOG2

    def cond(carry):
        step, live = carry
        return jnp.logical_and(step < qi, live)

    def body(carry):
        step, _ = carry
        blocks(qi - 1 - step, False)
        return step + 1, alive()

    lax.while_loop(cond, body, (jnp.int32(0), alive()))
    for c in range(n_par):
        o_ref[0, :, c * LANES:(c + 1) * LANES] = acc_sc[c].astype(BF16)


def _stick_breaking(q, k, v):
    b, s, width = q.shape
    blk, n_par = SB_BLOCK, SB_PAR
    q_spec = pl.BlockSpec((1, blk, n_par * LANES), lambda bi, h, qi: (bi, qi, h))
    kv_spec = pl.BlockSpec((1, s, n_par * LANES), lambda bi, h, qi: (bi, 0, h))
    return pl.pallas_call(
        functools.partial(_sb_kernel, blk=blk, n_par=n_par),
        grid=(b, width // (n_par * LANES), s // blk),
        in_specs=[q_spec, kv_spec, kv_spec],
        out_specs=q_spec,
        out_shape=jax.ShapeDtypeStruct(q.shape, BF16),
        scratch_shapes=[pltpu.VMEM((n_par, 2 * blk, LANES), F32), pltpu.VMEM((n_par, blk, LANES), F32)],
        compiler_params=_params(3),
        name="stick_breaking",
    )(q, k, v)


def _layer_norm(y, g, b):
    mu = jnp.mean(y, axis=1, keepdims=True)
    yc = y - mu
    var = jnp.mean(yc * yc, axis=1, keepdims=True)
    return yc * lax.rsqrt(var + LN_EPS) * g + b


def _first_index_of_max(vals, lane_f):
    m = jnp.max(vals, axis=1, keepdims=True)
    idx = jnp.min(jnp.where(vals == m, lane_f, float(LANES)), axis=1, keepdims=True)
    return m, idx


def _mix_ln_route_kernel(o_ref, x_ref, wo_ref, g_ref, b_ref, wr_hi_ref, wr_lo_ref, br_ref,
                         x1_ref, x1t_ref, mi_ref, mf_ref, cnt_ref, *, tm):
    step = pl.program_id(0)

    @pl.when(step == 0)
    def _():
        cnt_ref[...] = jnp.zeros(cnt_ref.shape, F32)

    mix = jnp.dot(o_ref[...], wo_ref[...], preferred_element_type=F32)
    x1 = _layer_norm(DEEPNORM_ALPHA * x_ref[...] + mix, g_ref[...], b_ref[...])
    x1_ref[...] = x1
    _store_token_tiles(x1t_ref, x1)

    xh = x1.astype(BF16)
    xl = (x1 - xh.astype(F32)).astype(BF16)
    logits = (jnp.dot(xh, wr_hi_ref[...], preferred_element_type=F32)
              + jnp.dot(xh, wr_lo_ref[...], preferred_element_type=F32)
              + jnp.dot(xl, wr_hi_ref[...], preferred_element_type=F32)) + br_ref[...]
    lane = lax.broadcasted_iota(I32, (tm, LANES), 1)
    lane_f = lane.astype(F32)
    is_group = lane < MOE_GROUPS
    g_max, g_idx = _first_index_of_max(jnp.where(is_group, logits, NEG_INF), lane_f)
    g_gate = 1.0 / jnp.sum(jnp.where(is_group, jnp.exp(logits - g_max), 0.0), axis=1, keepdims=True)
    first = MOE_GROUPS + MOE_EXPERTS_PER_GROUP * g_idx
    in_group = (lane_f >= first) & (lane_f < first + MOE_EXPERTS_PER_GROUP)
    e_logits = jnp.where(in_group, logits, NEG_INF)
    m1, i1 = _first_index_of_max(e_logits, lane_f)
    m2, i2 = _first_index_of_max(jnp.where(lane_f == i1, NEG_INF, e_logits), lane_f)
    ratio = jnp.exp(m2 - m1)
    w1 = g_gate / (1.0 + ratio)
    w2 = w1 * ratio

    e1 = jnp.where(lane_f == i1, 1.0, 0.0)
    e2 = jnp.where(lane_f == i2, 1.0, 0.0)
    both = e1 + e2
    rr = lax.broadcasted_iota(I32, (tm, tm), 0)
    cc = lax.broadcasted_iota(I32, (tm, tm), 1)
    earlier = jnp.where(rr > cc, 1.0, 0.0).astype(BF16)
    before = cnt_ref[...] + jnp.dot(earlier, both.astype(BF16), preferred_element_type=F32)
    rank1 = jnp.sum(e1 * before, axis=1, keepdims=True)
    rank2 = jnp.sum(e2 * before, axis=1, keepdims=True)
    cnt_ref[...] += jnp.sum(both, axis=0, keepdims=True)

    meta = jnp.where(lane == 0, i1 - MOE_GROUPS,
                     jnp.where(lane == 1, i2 - MOE_GROUPS,
                               jnp.where(lane == 2, rank1, jnp.where(lane == 3, rank2, 0.0))))
    mi_ref[...] = meta.T[:META_ROWS].astype(I32)
    mf_ref[...] = jnp.where(lane == 0, w1, jnp.where(lane == 1, w2, 0.0))


def _mix_ln_route(o2d, x2d, wo, ln_g, ln_b, w_group, b_group, w_expert, b_expert):
    t, d = x2d.shape
    tm = min(ROW_TILE, t)
    pad = LANES - MOE_GROUPS - N_EXPERTS
    wr = jnp.pad(jnp.concatenate([w_group, w_expert], axis=1).astype(F32), ((0, 0), (0, pad)))
    wr_hi = wr.astype(BF16)
    wr_lo = (wr - wr_hi.astype(F32)).astype(BF16)
    br = jnp.pad(jnp.concatenate([b_group, b_expert]).astype(F32), (0, pad)).reshape(1, LANES)
    row = lambda n: pl.BlockSpec((tm, n), lambda i: (i, 0))
    full = lambda r, c: pl.BlockSpec((r, c), lambda i: (0, 0))
    return pl.pallas_call(
        functools.partial(_mix_ln_route_kernel, tm=tm),
        grid=(t // tm,),
        in_specs=[row(o2d.shape[1]), row(d), full(o2d.shape[1], d), full(1, d), full(1, d),
                  full(d, LANES), full(d, LANES), full(1, LANES)],
        out_specs=[row(d), pl.BlockSpec((tm * ROW_GROUP, LANES), lambda i: (i, 0)),
                   pl.BlockSpec((META_ROWS, tm), lambda i: (0, i)), row(LANES), full(1, LANES)],
        out_shape=[jax.ShapeDtypeStruct((t, d), F32), jax.ShapeDtypeStruct((t * ROW_GROUP, LANES), F32),
                   jax.ShapeDtypeStruct((META_ROWS, t), I32),
                   jax.ShapeDtypeStruct((t, LANES), F32), jax.ShapeDtypeStruct((1, LANES), F32)],
        compiler_params=_params(1),
        name="mix_ln_route",
    )(o2d, x2d, wo.astype(BF16), ln_g.reshape(1, d).astype(F32), ln_b.reshape(1, d).astype(F32),
      wr_hi, wr_lo, br)


def _load_token_tiles(ref, rows):
    return jnp.concatenate([ref[pl.ds(s, rows, stride=ROW_GROUP), :] for s in range(ROW_GROUP)], axis=1)


def _store_token_tiles(ref, val):
    for s in range(ROW_GROUP):
        ref[pl.ds(s, val.shape[0], stride=ROW_GROUP), :] = val[:, s * LANES:(s + 1) * LANES]


def _start_row_copies(tm, copy_of):
    def group(g, carry):
        for r in range(ROW_GROUP):
            for slot in range(2):
                copy_of(g * ROW_GROUP + r, slot).start(priority=slot)
        return carry

    lax.fori_loop(0, tm // ROW_GROUP, group, 0)


def _wait_row_copies(tm, copy_of):
    def group(g, carry):
        for _ in range(2 * ROW_GROUP):
            copy_of(0, 0).wait()
        return carry

    lax.fori_loop(0, tm // ROW_GROUP, group, 0)


def _row(ref, row):
    return ref.at[pl.ds(pl.multiple_of(row * ROW_GROUP, ROW_GROUP), ROW_GROUP)]


def _dispatch_kernel(pad_base_ref, pad_len_ref, tail_ref, pos_ref, x_ref, xs_ref, zeros_sc, sem, *, tm, tile):
    @pl.when(pl.program_id(0) == 0)
    def _():
        zeros_sc[...] = jnp.zeros(zeros_sc.shape, F32)

        def pad_rows(e, carry):
            def one(r, c, wait):
                cp = pltpu.make_async_copy(_row(zeros_sc, 0), _row(xs_ref, pad_base_ref[e] + r), sem)
                cp.wait() if wait else cp.start()
                return c
            lax.fori_loop(0, pad_len_ref[e], functools.partial(one, wait=False), 0)
            lax.fori_loop(0, pad_len_ref[e], functools.partial(one, wait=True), 0)
            return carry

        lax.fori_loop(0, N_EXPERTS, pad_rows, 0)

        def tail_tile(i, carry):
            n = tile * ROW_GROUP
            cp = pltpu.make_async_copy(zeros_sc, xs_ref.at[pl.ds(pl.multiple_of(i * n, n), n)], sem)
            cp.start()
            cp.wait()
            return carry

        lax.fori_loop(tail_ref[0], tail_ref[1], tail_tile, 0)

    copy_of = lambda row, slot: pltpu.make_async_copy(_row(x_ref, row), _row(xs_ref, pos_ref[slot * tm + row]), sem)
    _start_row_copies(tm, copy_of)
    _wait_row_copies(tm, copy_of)


def _pos_spec(tm):
    return pl.BlockSpec((2 * tm,), lambda i, *prefetch: (i,), memory_space=pltpu.SMEM)


def _token_tile_spec(rows, index_map):
    return pl.BlockSpec((rows * ROW_GROUP, LANES), index_map)


def _dispatch(plan, x1t, n_rows):
    t = x1t.shape[0] // ROW_GROUP
    tm = min(ROW_TILE, t)
    grid_spec = pltpu.PrefetchScalarGridSpec(
        num_scalar_prefetch=3,
        grid=(t // tm,),
        in_specs=[_pos_spec(tm), _token_tile_spec(tm, lambda i, *prefetch: (i, 0))],
        out_specs=pl.BlockSpec(memory_space=pl.ANY),
        scratch_shapes=[pltpu.VMEM((FFN_TILE * ROW_GROUP, LANES), F32), pltpu.SemaphoreType.DMA(())],
    )
    return pl.pallas_call(
        functools.partial(_dispatch_kernel, tm=tm, tile=FFN_TILE),
        grid_spec=grid_spec,
        out_shape=jax.ShapeDtypeStruct((n_rows * ROW_GROUP, LANES), F32),
        compiler_params=_params(1),
        name="moe_dispatch",
    )(plan["pad_base"], plan["pad_len"], plan["tail"], plan["pos"], x1t)


def _ffn_kernel(tile_expert_ref, n_used_ref, xs_ref, w1_ref, w3_ref, w2_ref, ys_ref, w1_sc, w3_sc, w2_sc):
    i = pl.program_id(0)
    used = i < n_used_ref[0]
    new_expert = jnp.logical_or(i == 0, tile_expert_ref[i] != tile_expert_ref[jnp.maximum(i - 1, 0)])

    @pl.when(jnp.logical_and(used, new_expert))
    def _():
        w1_sc[...] = w1_ref[0, 0].astype(BF16)
        w3_sc[...] = w3_ref[0, 0].astype(BF16)
        w2_sc[...] = w2_ref[0, 0].astype(BF16)

    @pl.when(used)
    def _():
        x = _load_token_tiles(xs_ref, xs_ref.shape[0] // ROW_GROUP).astype(BF16)
        a = jnp.dot(x, w1_sc[...], preferred_element_type=F32)
        b = jnp.dot(x, w3_sc[...], preferred_element_type=F32)
        h = (a * jax.nn.sigmoid(a) * b).astype(BF16)
        _store_token_tiles(ys_ref, jnp.dot(h, w2_sc[...], preferred_element_type=F32))

    @pl.when(jnp.logical_not(used))
    def _():
        ys_ref[...] = jnp.zeros(ys_ref.shape, F32)


def _expert_ffn(plan, xs, w1, w3, w2, layer):
    n_rows = xs.shape[0] // ROW_GROUP
    d, hidden = w1.shape[2], w1.shape[3]
    assert d == ROW_GROUP * LANES
    tm = FFN_TILE
    x_spec = _token_tile_spec(tm, lambda i, te, nu: (jnp.minimum(i, nu[0] - 1), 0))
    w_in = pl.BlockSpec((1, 1, d, hidden), lambda i, te, nu: (layer, te[i], 0, 0))
    w_out = pl.BlockSpec((1, 1, hidden, d), lambda i, te, nu: (layer, te[i], 0, 0))
    grid_spec = pltpu.PrefetchScalarGridSpec(
        num_scalar_prefetch=2,
        grid=(n_rows // tm,),
        in_specs=[x_spec, w_in, w_in, w_out],
        out_specs=_token_tile_spec(tm, lambda i, te, nu: (i, 0)),
        scratch_shapes=[pltpu.VMEM((d, hidden), BF16), pltpu.VMEM((d, hidden), BF16),
                        pltpu.VMEM((hidden, d), BF16)],
    )
    return pl.pallas_call(
        _ffn_kernel,
        grid_spec=grid_spec,
        out_shape=jax.ShapeDtypeStruct(xs.shape, F32),
        compiler_params=_params(1),
        name="moe_experts",
    )(plan["tile_expert"], plan["n_used"], xs, w1.astype(F32), w3.astype(F32), w2.astype(F32))


def _combine_kernel(pos_ref, pos_next_ref, x1_ref, mf_ref, ys_ref, g_ref, b_ref, out_ref, ybuf, sem, *, tm):
    i = pl.program_id(0)
    cur = lax.rem(i, 2)

    def gather(positions, buf):
        return lambda row, slot: pltpu.make_async_copy(_row(ys_ref, positions[slot * tm + row]),
                                                       _row(ybuf.at[buf, slot], row), sem.at[buf])

    @pl.when(i == 0)
    def _():
        _start_row_copies(tm, gather(pos_ref, cur))

    @pl.when(i + 1 < pl.num_programs(0))
    def _():
        _start_row_copies(tm, gather(pos_next_ref, 1 - cur))

    _wait_row_copies(tm, gather(pos_ref, cur))
    gates = mf_ref[...]
    y = (gates[:, 0:1] * _load_token_tiles(ybuf.at[cur, 0], tm)
         + gates[:, 1:2] * _load_token_tiles(ybuf.at[cur, 1], tm))
    out_ref[...] = _layer_norm(DEEPNORM_ALPHA * x1_ref[...] + y, g_ref[...], b_ref[...])


def _combine(plan, x1, meta_f, ys, ln_g, ln_b):
    t, d = x1.shape
    tm = min(ROW_TILE, t)
    row = lambda n: pl.BlockSpec((tm, n), lambda i: (i, 0))
    full = pl.BlockSpec((1, d), lambda i: (0, 0))
    last = t // tm - 1
    pos_next = pl.BlockSpec((2 * tm,), lambda i: (jnp.minimum(i + 1, last),), memory_space=pltpu.SMEM)
    return pl.pallas_call(
        functools.partial(_combine_kernel, tm=tm),
        grid=(t // tm,),
        in_specs=[_pos_spec(tm), pos_next, row(d), row(LANES), pl.BlockSpec(memory_space=pl.ANY), full, full],
        out_specs=row(d),
        out_shape=jax.ShapeDtypeStruct((t, d), F32),
        scratch_shapes=[pltpu.VMEM((2, 2, tm * ROW_GROUP, LANES), F32), pltpu.SemaphoreType.DMA((2,))],
        compiler_params=_params(1),
        name="moe_combine",
    )(plan["pos"], plan["pos"], x1, meta_f, ys, ln_g.reshape(1, d).astype(F32), ln_b.reshape(1, d).astype(F32))


def _routing_plan(meta_i, counts_row, n_tiles):
    counts = counts_row[0, MOE_GROUPS:MOE_GROUPS + N_EXPERTS].astype(I32)
    padded = ((counts + FFN_TILE - 1) // FFN_TILE) * FFN_TILE
    ends = jnp.cumsum(padded)
    offsets = ends - padded
    experts = jnp.arange(N_EXPERTS, dtype=I32)
    expert, rank = meta_i[0:2], meta_i[2:4]
    pos = rank + jnp.sum(jnp.where(expert[..., None] == experts, offsets, 0), axis=-1)
    tile_ends = ends // FFN_TILE
    tile_expert = jnp.sum(jnp.arange(n_tiles, dtype=I32)[:, None] >= tile_ends[None, :], axis=1)
    n_used = tile_ends[-1:].astype(I32)
    tm = min(ROW_TILE, pos.shape[1])
    pos = pos.astype(I32).reshape(2, -1, tm).transpose(1, 0, 2).reshape(-1)
    return dict(pos=pos, tile_expert=jnp.minimum(tile_expert, N_EXPERTS - 1).astype(I32),
                n_used=n_used, pad_base=(offsets + counts).astype(I32), pad_len=(padded - counts).astype(I32),
                tail=jnp.concatenate([n_used, jnp.full((1,), n_tiles, I32)]))


def _hier_moe_ln(x1, x1t, meta_i, meta_f, counts_row, w1, w3, w2, layer, ln_g, ln_b):
    t, d = x1.shape
    n_tiles = (2 * t) // FFN_TILE + N_EXPERTS
    plan = _routing_plan(meta_i, counts_row, n_tiles)
    xs = _dispatch(plan, x1t, n_tiles * FFN_TILE)
    ys = _expert_ffn(plan, xs, w1, w3, w2, layer)
    return _combine(plan, x1, meta_f, ys, ln_g, ln_b)


def kernel(x, rel_table, da_wq, da_wk, da_wv, da_wo, da_lq1, da_lk1, da_lq2, da_lk2, da_subln_g, sb_wq, sb_wk,
           sb_wv, sb_wo, ln_mix_g, ln_mix_b, ln_ffn_g, ln_ffn_b, moe_w_group, moe_b_group, moe_w_expert,
           moe_b_expert, moe_w1, moe_w3, moe_w2):
    b, s, d = x.shape
    t = b * s
    assert s % DA_BLOCK == 0 and s % SB_BLOCK == 0 and t % min(ROW_TILE, t) == 0 and (2 * t) % FFN_TILE == 0
    xf = x.reshape(t, d).astype(F32)
    bias = _bias_tiles(rel_table, DA_BLOCK)
    for layer in range(DEPTH):
        j = layer // 2
        if layer % 2 == 0:
            q, k, v = _qkv_proj(xf, da_wq[j], da_wk[j], da_wv[j], DA_HEAD_DIM ** -0.5 * LOG2E)
            shape = (b, s, q.shape[1])
            lam_init = 0.8 - 0.6 * math.exp(-0.3 * layer)
            o = _diff_attention(q.reshape(shape), k.reshape(shape), v.reshape(shape), bias,
                                da_lq1[j], da_lk1[j], da_lq2[j], da_lk2[j], da_subln_g[j], lam_init)
            wo = da_wo[j]
        else:
            q, k, v = _qkv_proj(xf, sb_wq[j], sb_wk[j], sb_wv[j], -(SB_HEAD_DIM ** -0.5) * LOG2E)
            shape = (b, s, q.shape[1])
            o = _stick_breaking(q.reshape(shape), k.reshape(shape), v.reshape(shape))
            wo = sb_wo[j]
        x1, x1t, meta_i, meta_f, counts = _mix_ln_route(
            o.reshape(t, -1), xf, wo, ln_mix_g[layer], ln_mix_b[layer],
            moe_w_group[layer], moe_b_group[layer], moe_w_expert[layer], moe_b_expert[layer])
        xf = _hier_moe_ln(x1, x1t, meta_i, meta_f, counts, moe_w1, moe_w3, moe_w2, layer,
                          ln_ffn_g[layer], ln_ffn_b[layer])
    return xf.reshape(b, s, d).astype(x.dtype)
```

```python
import functools
import math

import jax
import jax.numpy as jnp
from jax import lax
from jax.experimental import pallas as pl
from jax.experimental.pallas import tpu as pltpu

F32 = jnp.float32
BF16 = jnp.bfloat16
I32 = jnp.int32

DEPTH = 2
DA_HEADS = 8
DA_HEAD_DIM = 64
SB_HEAD_DIM = 64
REL_BUCKETS = 32
REL_MAX_DIST = 128
MOE_GROUPS = 4
MOE_EXPERTS_PER_GROUP = 8
N_EXPERTS = MOE_GROUPS * MOE_EXPERTS_PER_GROUP
DEEPNORM_ALPHA = (2 * DEPTH) ** 0.25
LN_EPS = 1e-5
RMS_EPS = 1e-6
NEG_INF = -1e30
LOG2E = 1.4426950408889634
SB_DEAD_LOG2 = -160.0

LANES = 128
DA_BLOCK = 512
DA_PAR = 2
SB_BLOCK = 256
SB_PAR = 4
ROW_TILE = 512
DISPATCH_TILE = 1024
FFN_TILE = 256
META_ROWS = 8
ROW_GROUP = 8
VMEM_LIMIT = 48 * 1024 * 1024


def _params(n_axes, vmem=VMEM_LIMIT):
    return pltpu.CompilerParams(dimension_semantics=("arbitrary",) * n_axes, vmem_limit_bytes=vmem)


def _proj_kernel(x_ref, wq_ref, wk_ref, wv_ref, q_ref, k_ref, v_ref, *, q_scale):
    x = x_ref[...].astype(BF16)
    q = jnp.dot(x, wq_ref[...], preferred_element_type=F32)
    q_ref[...] = (q * q_scale).astype(BF16)
    k_ref[...] = jnp.dot(x, wk_ref[...], preferred_element_type=F32).astype(BF16)
    v_ref[...] = jnp.dot(x, wv_ref[...], preferred_element_type=F32).astype(BF16)


def _qkv_proj(x2d, wq, wk, wv, q_scale):
    t, d = x2d.shape
    n = wq.shape[1]
    tm = min(ROW_TILE, t)
    row_in = pl.BlockSpec((tm, d), lambda i: (i, 0))
    w_spec = pl.BlockSpec((d, n), lambda i: (0, 0))
    row_out = pl.BlockSpec((tm, n), lambda i: (i, 0))
    return pl.pallas_call(
        functools.partial(_proj_kernel, q_scale=q_scale),
        grid=(t // tm,),
        in_specs=[row_in, w_spec, w_spec, w_spec],
        out_specs=[row_out, row_out, row_out],
        out_shape=[jax.ShapeDtypeStruct((t, n), BF16)] * 3,
        compiler_params=_params(1),
        name="qkv_proj",
    )(x2d, wq.astype(BF16), wk.astype(BF16), wv.astype(BF16))


def _bias_kernel(tab_ref, out_ref, *, blk):
    h = pl.program_id(0)
    i = lax.broadcasted_iota(I32, (blk, 2 * blk), 0)
    j = lax.broadcasted_iota(I32, (blk, 2 * blk), 1)
    n = blk + i - j
    nn = jnp.maximum(n, 0)
    max_exact = REL_BUCKETS // 2
    nf = jnp.maximum(nn, 1).astype(F32)
    large = max_exact + (jnp.log(nf * (1.0 / max_exact)) * ((REL_BUCKETS - max_exact) / math.log(REL_MAX_DIST / max_exact))
                         ).astype(I32)
    large = jnp.minimum(large, REL_BUCKETS - 1)
    bucket = jnp.where(nn < max_exact, nn, large)
    far = tab_ref[REL_BUCKETS - 1, h]
    acc = jnp.zeros((blk, 2 * blk), F32)
    for b in range(REL_BUCKETS):
        acc = jnp.where(bucket == b, tab_ref[b, h] - far, acc)
    out_ref[0] = jnp.where(n >= 0, acc * LOG2E, NEG_INF)


def _bias_tiles(rel_table, blk):
    heads = rel_table.shape[1]
    return pl.pallas_call(
        functools.partial(_bias_kernel, blk=blk),
        grid=(heads,),
        in_specs=[pl.BlockSpec(memory_space=pltpu.SMEM)],
        out_specs=pl.BlockSpec((1, blk, 2 * blk), lambda h: (h, 0, 0)),
        out_shape=jax.ShapeDtypeStruct((heads, blk, 2 * blk), F32),
        compiler_params=_params(1),
        name="rel_bias_tiles",
    )(rel_table.astype(F32))


def _split_streams(q):
    lane = lax.broadcasted_iota(I32, q.shape, 1)
    zero = jnp.zeros_like(q)
    return jnp.concatenate([jnp.where(lane < 64, q, zero), jnp.where(lane >= 64, q, zero)], axis=0)


def _da_kernel(q_ref, k_ref, v_ref, bias_ref, lq1_ref, lk1_ref, lq2_ref, lk2_ref, g_ref, o_ref,
               m_sc, l_sc, acc_sc, s_sc, *, blk, lam_init, n_par):
    qi = pl.program_id(2)
    heads = range(n_par)
    qs = [_split_streams(q_ref[0, :, c * LANES:(c + 1) * LANES]) for c in heads]
    m_sc[...] = jnp.full(m_sc.shape, NEG_INF, F32)
    l_sc[...] = jnp.zeros(l_sc.shape, F32)
    acc_sc[...] = jnp.zeros(acc_sc.shape, F32)

    def logits(ki):
        start = pl.multiple_of(ki * blk, blk)
        return [lax.dot_general(qs[c], k_ref[0, pl.ds(start, blk), c * LANES:(c + 1) * LANES],
                                (((1,), (1,)), ((), ())), preferred_element_type=F32) for c in heads]

    def consume(ki, bias_cols):
        start = pl.multiple_of(ki * blk, blk)
        alpha, p = [], []
        for c in heads:
            sc = s_sc[c]
            if bias_cols is not None:
                bias = bias_ref[c, :, bias_cols]
                sc = sc + jnp.concatenate([bias, bias], axis=0)
            m_prev = m_sc[c]
            m_next = jnp.maximum(m_prev, jnp.max(sc, axis=1, keepdims=True))
            m_sc[c] = m_next
            alpha.append(jnp.exp2(m_prev - m_next))
            pc = jnp.exp2(sc - jnp.concatenate([m_next] * (blk // LANES), axis=1))
            l_sc[c] = alpha[c] * l_sc[c] + sum(pc[:, n * LANES:(n + 1) * LANES] for n in range(blk // LANES))
            p.append(pc.astype(BF16))
        pv = [jnp.dot(p[c], v_ref[0, pl.ds(start, blk), c * LANES:(c + 1) * LANES],
                      preferred_element_type=F32) for c in heads]
        for c in heads:
            acc_sc[c] = alpha[c] * acc_sc[c] + pv[c]

    def step(ki, bias_cols):
        nxt = logits(ki + 1)
        consume(ki, bias_cols)
        for c in heads:
            s_sc[c] = nxt[c]

    first = logits(0)
    for c in heads:
        s_sc[c] = first[c]

    def far_body(ki, carry):
        step(ki, None)
        return carry

    lax.fori_loop(0, jnp.maximum(qi - 1, 0), far_body, 0)

    @pl.when(qi >= 1)
    def _():
        step(qi - 1, slice(0, blk))

    consume(qi, slice(blk, 2 * blk))

    lam = (jnp.exp(jnp.sum(lq1_ref[...] * lk1_ref[...], axis=1, keepdims=True))
           - jnp.exp(jnp.sum(lq2_ref[...] * lk2_ref[...], axis=1, keepdims=True)) + lam_init)
    for c in heads:
        inv_l = 1.0 / jnp.sum(l_sc[c], axis=1, keepdims=True)
        acc = acc_sc[c]
        o = acc[:blk] * inv_l[:blk] - lam * (acc[blk:] * inv_l[blk:])
        o = o * lax.rsqrt(jnp.mean(o * o, axis=1, keepdims=True) + RMS_EPS) * g_ref[...]
        o_ref[0, :, c * LANES:(c + 1) * LANES] = (o * (1.0 - lam_init)).astype(BF16)


def _diff_attention(q, k, v, bias, lq1, lk1, lq2, lk2, subln_g, lam_init):
    b, s, _ = q.shape
    blk, n_par = DA_BLOCK, DA_PAR
    q_spec = pl.BlockSpec((1, blk, n_par * LANES), lambda bi, h, qi: (bi, qi, h))
    kv_spec = pl.BlockSpec((1, s, n_par * LANES), lambda bi, h, qi: (bi, 0, h))
    vec64 = pl.BlockSpec((1, DA_HEAD_DIM), lambda bi, h, qi: (0, 0))
    state = pltpu.VMEM((n_par, 2 * blk, LANES), F32)
    return pl.pallas_call(
        functools.partial(_da_kernel, blk=blk, lam_init=lam_init, n_par=n_par),
        grid=(b, DA_HEADS // n_par, s // blk),
        in_specs=[q_spec, kv_spec, kv_spec,
                  pl.BlockSpec((n_par, blk, 2 * blk), lambda bi, h, qi: (h, 0, 0)),
                  vec64, vec64, vec64, vec64,
                  pl.BlockSpec((1, LANES), lambda bi, h, qi: (0, 0))],
        out_specs=q_spec,
        out_shape=jax.ShapeDtypeStruct(q.shape, BF16),
        scratch_shapes=[state, state, state, pltpu.VMEM((n_par, 2 * blk, blk), F32)],
        compiler_params=_params(3),
        name="diff_attention",
    )(q, k, v, bias, lq1.reshape(1, -1), lk1.reshape(1, -1), lq2.reshape(1, -1), lk2.reshape(1, -1),
      subln_g.reshape(1, -1))


def _sb_kernel(q_ref, k_ref, v_ref, o_ref, r_sc, acc_sc, *, blk, n_par):
    qi = pl.program_id(2)
    rows = 2 * blk
    qs = [_split_streams(q_ref[0, :, c * LANES:(c + 1) * LANES]) for c in range(n_par)]
    jj = lax.broadcasted_iota(I32, (blk, blk), 0)
    ss = lax.broadcasted_iota(I32, (blk, blk), 1)
    later = jnp.where(jj > ss, 1.0, 0.0).astype(BF16)
    later2 = jnp.concatenate([later, later], axis=0)
    lane = lax.broadcasted_iota(I32, (blk, LANES), 1)
    r_sc[...] = jnp.zeros(r_sc.shape, F32)
    acc_sc[...] = jnp.zeros(acc_sc.shape, F32)

    def blocks(ki, diagonal):
        start = pl.multiple_of(ki * blk, blk)
        pairs = range(n_par)
        zn = [lax.dot_general(qs[c], k_ref[0, pl.ds(start, blk), c * LANES:(c + 1) * LANES],
                              (((1,), (1,)), ((), ())), preferred_element_type=F32) for c in pairs]
        if diagonal:
            r = lax.broadcasted_iota(I32, (rows, blk), 0)
            t = jnp.where(r >= blk, r - blk, r)
            strict = lax.broadcasted_iota(I32, (rows, blk), 1) < t
        log_fail, log_beta, hi_lo = [], [], []
        for c in pairs:
            neg_abs = pltpu.bitcast(pltpu.bitcast(zn[c], jnp.uint32) | jnp.uint32(0x80000000), F32)
            softplus = jnp.log(1.0 + jnp.exp2(neg_abs)) * LOG2E
            lf = jnp.minimum(zn[c], 0.0) - softplus
            log_beta.append(lf - zn[c])
            if diagonal:
                lf = jnp.where(strict, lf, 0.0)
            log_fail.append(lf)
            hi = pltpu.bitcast(pltpu.bitcast(lf, jnp.uint32) & jnp.uint32(0xFFFF0000), F32)
            hi_lo.append(jnp.concatenate([hi.astype(BF16), (lf - hi).astype(BF16)], axis=1))
        suffix = [jnp.dot(hi_lo[c], later2, preferred_element_type=F32) for c in pairs]
        wb = []
        for c in pairs:
            w = jnp.exp2(log_beta[c] + suffix[c])
            if diagonal:
                w = jnp.where(strict, w, 0.0)
            wb.append(w.astype(BF16))
        pv = []
        for c in pairs:
            vb = v_ref[0, pl.ds(start, blk), c * LANES:(c + 1) * LANES]
            pv.append((jnp.dot(wb[c][:blk], vb, preferred_element_type=F32),
                       jnp.dot(wb[c][blk:], vb, preferred_element_type=F32)))
        for c in pairs:
            carry = jnp.exp2(r_sc[c])
            acc_sc[c] += jnp.where(lane < 64, carry[:blk] * pv[c][0], carry[blk:] * pv[c][1])
            r_sc[c] += suffix[c][:, 0:1] + log_fail[c][:, 0:1]

    blocks(qi, True)

    def alive():
        return jnp.max(r_sc[...]) > SB_DEAD_LOG2

    def cond(carry):
        step, live = carry
        return jnp.logical_and(step < qi, live)

    def body(carry):
        step, _ = carry
        blocks(qi - 1 - step, False)
        return step + 1, alive()

    lax.while_loop(cond, body, (jnp.int32(0), alive()))
    for c in range(n_par):
        o_ref[0, :, c * LANES:(c + 1) * LANES] = acc_sc[c].astype(BF16)


def _stick_breaking(q, k, v):
    b, s, width = q.shape
    blk, n_par = SB_BLOCK, SB_PAR
    q_spec = pl.BlockSpec((1, blk, n_par * LANES), lambda bi, h, qi: (bi, qi, h))
    kv_spec = pl.BlockSpec((1, s, n_par * LANES), lambda bi, h, qi: (bi, 0, h))
    return pl.pallas_call(
        functools.partial(_sb_kernel, blk=blk, n_par=n_par),
        grid=(b, width // (n_par * LANES), s // blk),
        in_specs=[q_spec, kv_spec, kv_spec],
        out_specs=q_spec,
        out_shape=jax.ShapeDtypeStruct(q.shape, BF16),
        scratch_shapes=[pltpu.VMEM((n_par, 2 * blk, LANES), F32), pltpu.VMEM((n_par, blk, LANES), F32)],
        compiler_params=_params(3),
        name="stick_breaking",
    )(q, k, v)


def _layer_norm(y, g, b):
    mu = jnp.mean(y, axis=1, keepdims=True)
    yc = y - mu
    var = jnp.mean(yc * yc, axis=1, keepdims=True)
    return yc * lax.rsqrt(var + LN_EPS) * g + b


def _first_index_of_max(vals, lane_f):
    m = jnp.max(vals, axis=1, keepdims=True)
    idx = jnp.min(jnp.where(vals == m, lane_f, float(LANES)), axis=1, keepdims=True)
    return m, idx


def _top_experts(logits):
    lane_f = lax.broadcasted_iota(I32, logits.shape, 1).astype(F32)
    is_group = lane_f < MOE_GROUPS
    g_max, g_idx = _first_index_of_max(jnp.where(is_group, logits, NEG_INF), lane_f)
    g_gate = 1.0 / jnp.sum(jnp.where(is_group, jnp.exp(logits - g_max), 0.0), axis=1, keepdims=True)
    first = MOE_GROUPS + MOE_EXPERTS_PER_GROUP * g_idx
    in_group = (lane_f >= first) & (lane_f < first + MOE_EXPERTS_PER_GROUP)
    e_logits = jnp.where(in_group, logits, NEG_INF)
    m1, i1 = _first_index_of_max(e_logits, lane_f)
    m2, i2 = _first_index_of_max(jnp.where(lane_f == i1, NEG_INF, e_logits), lane_f)
    ratio = jnp.exp(m2 - m1)
    w1 = g_gate / (1.0 + ratio)
    return i1, i2, w1, w1 * ratio


def _mix_ln_route_kernel(o_ref, x_ref, wo_ref, g_ref, b_ref, wr_hi_ref, wr_lo_ref, br_ref,
                         x1_ref, x1t_ref, mi_ref, mf_ref, cnt_ref, *, tm):
    step = pl.program_id(0)

    @pl.when(step == 0)
    def _():
        cnt_ref[...] = jnp.zeros(cnt_ref.shape, F32)

    mix = jnp.dot(o_ref[...], wo_ref[...], preferred_element_type=F32)
    x1 = _layer_norm(DEEPNORM_ALPHA * x_ref[...] + mix, g_ref[...], b_ref[...])
    x1_ref[...] = x1
    _store_token_tiles(x1t_ref, x1)

    xh = x1.astype(BF16)
    xl = (x1 - xh.astype(F32)).astype(BF16)
    logits = (jnp.dot(xh, wr_hi_ref[...], preferred_element_type=F32)
              + jnp.dot(xh, wr_lo_ref[...], preferred_element_type=F32)
              + jnp.dot(xl, wr_hi_ref[...], preferred_element_type=F32)) + br_ref[...]
    i1, i2, w1, w2 = _top_experts(logits)

    lane = lax.broadcasted_iota(I32, (tm, LANES), 1)
    lane_f = lane.astype(F32)
    e1 = jnp.where(lane_f == i1, 1.0, 0.0)
    e2 = jnp.where(lane_f == i2, 1.0, 0.0)
    both = e1 + e2
    rr = lax.broadcasted_iota(I32, (tm, tm), 0)
    cc = lax.broadcasted_iota(I32, (tm, tm), 1)
    earlier = jnp.where(rr > cc, 1.0, 0.0).astype(BF16)
    before = cnt_ref[...] + jnp.dot(earlier, both.astype(BF16), preferred_element_type=F32)
    rank1 = jnp.sum(e1 * before, axis=1, keepdims=True)
    rank2 = jnp.sum(e2 * before, axis=1, keepdims=True)
    cnt_ref[...] += jnp.sum(both, axis=0, keepdims=True)

    meta = jnp.where(lane == 0, i1 - MOE_GROUPS,
                     jnp.where(lane == 1, i2 - MOE_GROUPS,
                               jnp.where(lane == 2, rank1, jnp.where(lane == 3, rank2, 0.0))))
    mi_ref[...] = meta.T[:META_ROWS].astype(I32)
    mf_ref[...] = jnp.where(lane == 0, w1, jnp.where(lane == 1, w2, 0.0))


def _mix_ln_route(o2d, x2d, wo, ln_g, ln_b, w_group, b_group, w_expert, b_expert):
    t, d = x2d.shape
    tm = min(ROW_TILE, t)
    pad = LANES - MOE_GROUPS - N_EXPERTS
    wr = jnp.pad(jnp.concatenate([w_group, w_expert], axis=1).astype(F32), ((0, 0), (0, pad)))
    wr_hi = wr.astype(BF16)
    wr_lo = (wr - wr_hi.astype(F32)).astype(BF16)
    br = jnp.pad(jnp.concatenate([b_group, b_expert]).astype(F32), (0, pad)).reshape(1, LANES)
    row = lambda n: pl.BlockSpec((tm, n), lambda i: (i, 0))
    full = lambda r, c: pl.BlockSpec((r, c), lambda i: (0, 0))
    return pl.pallas_call(
        functools.partial(_mix_ln_route_kernel, tm=tm),
        grid=(t // tm,),
        in_specs=[row(o2d.shape[1]), row(d), full(o2d.shape[1], d), full(1, d), full(1, d),
                  full(d, LANES), full(d, LANES), full(1, LANES)],
        out_specs=[row(d), pl.BlockSpec((tm * ROW_GROUP, LANES), lambda i: (i, 0)),
                   pl.BlockSpec((META_ROWS, tm), lambda i: (0, i)), row(LANES), full(1, LANES)],
        out_shape=[jax.ShapeDtypeStruct((t, d), F32), jax.ShapeDtypeStruct((t * ROW_GROUP, LANES), F32),
                   jax.ShapeDtypeStruct((META_ROWS, t), I32),
                   jax.ShapeDtypeStruct((t, LANES), F32), jax.ShapeDtypeStruct((1, LANES), F32)],
        compiler_params=_params(1),
        name="mix_ln_route",
    )(o2d, x2d, wo.astype(BF16), ln_g.reshape(1, d).astype(F32), ln_b.reshape(1, d).astype(F32),
      wr_hi, wr_lo, br)


def _load_token_tiles(ref, rows):
    return jnp.concatenate([ref[pl.ds(s, rows, stride=ROW_GROUP), :] for s in range(ROW_GROUP)], axis=1)


def _store_token_tiles(ref, val):
    for s in range(ROW_GROUP):
        ref[pl.ds(s, val.shape[0], stride=ROW_GROUP), :] = val[:, s * LANES:(s + 1) * LANES]


def _start_row_copies(tm, copy_of):
    def group(g, carry):
        for r in range(ROW_GROUP):
            for slot in range(2):
                copy_of(g * ROW_GROUP + r, slot).start(priority=slot)
        return carry

    lax.fori_loop(0, tm // ROW_GROUP, group, 0)


def _wait_row_copies(tm, copy_of):
    def group(g, carry):
        for _ in range(2 * ROW_GROUP):
            copy_of(0, 0).wait()
        return carry

    lax.fori_loop(0, tm // ROW_GROUP, group, 0)


def _row(ref, row):
    return ref.at[pl.ds(pl.multiple_of(row * ROW_GROUP, ROW_GROUP), ROW_GROUP)]


def _dispatch_kernel(pad_base_ref, pad_len_ref, tail_ref, pos_ref, x_ref, xs_ref, zeros_sc, sem, *, tm, tile):
    @pl.when(pl.program_id(0) == 0)
    def _():
        zeros_sc[...] = jnp.zeros(zeros_sc.shape, F32)

        def pad_rows(e, carry):
            def one(r, c, wait):
                cp = pltpu.make_async_copy(_row(zeros_sc, 0), _row(xs_ref, pad_base_ref[e] + r), sem)
                cp.wait() if wait else cp.start()
                return c
            lax.fori_loop(0, pad_len_ref[e], functools.partial(one, wait=False), 0)
            lax.fori_loop(0, pad_len_ref[e], functools.partial(one, wait=True), 0)
            return carry

        lax.fori_loop(0, N_EXPERTS, pad_rows, 0)

        def tail_tile(i, carry):
            n = tile * ROW_GROUP
            cp = pltpu.make_async_copy(zeros_sc, xs_ref.at[pl.ds(pl.multiple_of(i * n, n), n)], sem)
            cp.start()
            cp.wait()
            return carry

        lax.fori_loop(tail_ref[0], tail_ref[1], tail_tile, 0)

    copy_of = lambda row, slot: pltpu.make_async_copy(_row(x_ref, row), _row(xs_ref, pos_ref[slot * tm + row]), sem)
    _start_row_copies(tm, copy_of)
    _wait_row_copies(tm, copy_of)


def _pos_spec(tm):
    return pl.BlockSpec((2 * tm,), lambda i, *prefetch: (i,), memory_space=pltpu.SMEM)


def _token_tile_spec(rows, index_map):
    return pl.BlockSpec((rows * ROW_GROUP, LANES), index_map)


def _dispatch(plan, x1t, n_rows):
    t = x1t.shape[0] // ROW_GROUP
    tm = min(DISPATCH_TILE, t)
    grid_spec = pltpu.PrefetchScalarGridSpec(
        num_scalar_prefetch=3,
        grid=(t // tm,),
        in_specs=[_pos_spec(tm), _token_tile_spec(tm, lambda i, *prefetch: (i, 0))],
        out_specs=pl.BlockSpec(memory_space=pl.ANY),
        scratch_shapes=[pltpu.VMEM((FFN_TILE * ROW_GROUP, LANES), F32), pltpu.SemaphoreType.DMA(())],
    )
    return pl.pallas_call(
        functools.partial(_dispatch_kernel, tm=tm, tile=FFN_TILE),
        grid_spec=grid_spec,
        out_shape=jax.ShapeDtypeStruct((n_rows * ROW_GROUP, LANES), F32),
        compiler_params=_params(1),
        name="moe_dispatch",
    )(plan["pad_base"], plan["pad_len"], plan["tail"], plan["pos_dispatch"], x1t)


def _ffn_kernel(tile_expert_ref, n_used_ref, xs_ref, w1_ref, w3_ref, w2_ref, ys_ref, w1_sc, w3_sc, w2_sc):
    i = pl.program_id(0)
    used = i < n_used_ref[0]
    new_expert = jnp.logical_or(i == 0, tile_expert_ref[i] != tile_expert_ref[jnp.maximum(i - 1, 0)])

    @pl.when(jnp.logical_and(used, new_expert))
    def _():
        w1_sc[...] = w1_ref[0, 0].astype(BF16)
        w3_sc[...] = w3_ref[0, 0].astype(BF16)
        w2_sc[...] = w2_ref[0, 0].astype(BF16)

    @pl.when(used)
    def _():
        x = _load_token_tiles(xs_ref, xs_ref.shape[0] // ROW_GROUP).astype(BF16)
        a = jnp.dot(x, w1_sc[...], preferred_element_type=F32)
        b = jnp.dot(x, w3_sc[...], preferred_element_type=F32)
        h = (a * jax.nn.sigmoid(a) * b).astype(BF16)
        _store_token_tiles(ys_ref, jnp.dot(h, w2_sc[...], preferred_element_type=F32))

    @pl.when(jnp.logical_not(used))
    def _():
        ys_ref[...] = jnp.zeros(ys_ref.shape, F32)


def _expert_ffn(plan, xs, w1, w3, w2, layer):
    n_rows = xs.shape[0] // ROW_GROUP
    d, hidden = w1.shape[2], w1.shape[3]
    assert d == ROW_GROUP * LANES
    tm = FFN_TILE
    x_spec = _token_tile_spec(tm, lambda i, te, nu: (jnp.minimum(i, nu[0] - 1), 0))
    w_in = pl.BlockSpec((1, 1, d, hidden), lambda i, te, nu: (layer, te[i], 0, 0))
    w_out = pl.BlockSpec((1, 1, hidden, d), lambda i, te, nu: (layer, te[i], 0, 0))
    grid_spec = pltpu.PrefetchScalarGridSpec(
        num_scalar_prefetch=2,
        grid=(n_rows // tm,),
        in_specs=[x_spec, w_in, w_in, w_out],
        out_specs=_token_tile_spec(tm, lambda i, te, nu: (i, 0)),
        scratch_shapes=[pltpu.VMEM((d, hidden), BF16), pltpu.VMEM((d, hidden), BF16),
                        pltpu.VMEM((hidden, d), BF16)],
    )
    return pl.pallas_call(
        _ffn_kernel,
        grid_spec=grid_spec,
        out_shape=jax.ShapeDtypeStruct(xs.shape, F32),
        compiler_params=_params(1),
        name="moe_experts",
    )(plan["tile_expert"], plan["n_used"], xs, w1.astype(F32), w3.astype(F32), w2.astype(F32))


def _combine_kernel(pos_ref, pos_next_ref, x1_ref, mf_ref, ys_ref, g_ref, b_ref, out_ref, ybuf, sem, *, tm):
    i = pl.program_id(0)
    cur = lax.rem(i, 2)

    def gather(positions, buf):
        return lambda row, slot: pltpu.make_async_copy(_row(ys_ref, positions[slot * tm + row]),
                                                       _row(ybuf.at[buf, slot], row), sem.at[buf])

    @pl.when(i == 0)
    def _():
        _start_row_copies(tm, gather(pos_ref, cur))

    @pl.when(i + 1 < pl.num_programs(0))
    def _():
        _start_row_copies(tm, gather(pos_next_ref, 1 - cur))

    _wait_row_copies(tm, gather(pos_ref, cur))
    gates = mf_ref[...]
    y = (gates[:, 0:1] * _load_token_tiles(ybuf.at[cur, 0], tm)
         + gates[:, 1:2] * _load_token_tiles(ybuf.at[cur, 1], tm))
    out_ref[...] = _layer_norm(DEEPNORM_ALPHA * x1_ref[...] + y, g_ref[...], b_ref[...])


def _combine(plan, x1, meta_f, ys, ln_g, ln_b):
    t, d = x1.shape
    tm = min(ROW_TILE, t)
    row = lambda n: pl.BlockSpec((tm, n), lambda i: (i, 0))
    full = pl.BlockSpec((1, d), lambda i: (0, 0))
    last = t // tm - 1
    pos_next = pl.BlockSpec((2 * tm,), lambda i: (jnp.minimum(i + 1, last),), memory_space=pltpu.SMEM)
    return pl.pallas_call(
        functools.partial(_combine_kernel, tm=tm),
        grid=(t // tm,),
        in_specs=[_pos_spec(tm), pos_next, row(d), row(LANES), pl.BlockSpec(memory_space=pl.ANY), full, full],
        out_specs=row(d),
        out_shape=jax.ShapeDtypeStruct((t, d), F32),
        scratch_shapes=[pltpu.VMEM((2, 2, tm * ROW_GROUP, LANES), F32), pltpu.SemaphoreType.DMA((2,))],
        compiler_params=_params(1),
        name="moe_combine",
    )(plan["pos"], plan["pos"], x1, meta_f, ys, ln_g.reshape(1, d).astype(F32), ln_b.reshape(1, d).astype(F32))


def _routing_plan(meta_i, counts_row, n_tiles):
    counts = counts_row[0, MOE_GROUPS:MOE_GROUPS + N_EXPERTS].astype(I32)
    padded = ((counts + FFN_TILE - 1) // FFN_TILE) * FFN_TILE
    ends = jnp.cumsum(padded)
    offsets = ends - padded
    experts = jnp.arange(N_EXPERTS, dtype=I32)
    expert, rank = meta_i[0:2], meta_i[2:4]
    pos = rank + jnp.sum(jnp.where(expert[..., None] == experts, offsets, 0), axis=-1)
    tile_ends = ends // FFN_TILE
    tile_expert = jnp.sum(jnp.arange(n_tiles, dtype=I32)[:, None] >= tile_ends[None, :], axis=1)
    n_used = tile_ends[-1:].astype(I32)
    pos = pos.astype(I32)
    t = pos.shape[1]
    by_tile = lambda tm: pos.reshape(2, -1, tm).transpose(1, 0, 2).reshape(-1)
    return dict(pos=by_tile(min(ROW_TILE, t)), pos_dispatch=by_tile(min(DISPATCH_TILE, t)), tile_expert=jnp.minimum(tile_expert, N_EXPERTS - 1).astype(I32),
                n_used=n_used, pad_base=(offsets + counts).astype(I32), pad_len=(padded - counts).astype(I32),
                tail=jnp.concatenate([n_used, jnp.full((1,), n_tiles, I32)]))


def _hier_moe_ln(x1, x1t, meta_i, meta_f, counts_row, w1, w3, w2, layer, ln_g, ln_b):
    t, d = x1.shape
    n_tiles = (2 * t) // FFN_TILE + N_EXPERTS
    plan = _routing_plan(meta_i, counts_row, n_tiles)
    xs = _dispatch(plan, x1t, n_tiles * FFN_TILE)
    ys = _expert_ffn(plan, xs, w1, w3, w2, layer)
    return _combine(plan, x1, meta_f, ys, ln_g, ln_b)


def kernel(x, rel_table, da_wq, da_wk, da_wv, da_wo, da_lq1, da_lk1, da_lq2, da_lk2, da_subln_g, sb_wq, sb_wk,
           sb_wv, sb_wo, ln_mix_g, ln_mix_b, ln_ffn_g, ln_ffn_b, moe_w_group, moe_b_group, moe_w_expert,
           moe_b_expert, moe_w1, moe_w3, moe_w2):
    b, s, d = x.shape
    t = b * s
    assert s % DA_BLOCK == 0 and s % SB_BLOCK == 0 and (2 * t) % FFN_TILE == 0
    assert t % min(ROW_TILE, t) == 0 and t % min(DISPATCH_TILE, t) == 0
    xf = x.reshape(t, d).astype(F32)
    bias = _bias_tiles(rel_table, DA_BLOCK)
    for layer in range(DEPTH):
        j = layer // 2
        if layer % 2 == 0:
            q, k, v = _qkv_proj(xf, da_wq[j], da_wk[j], da_wv[j], DA_HEAD_DIM ** -0.5 * LOG2E)
            shape = (b, s, q.shape[1])
            lam_init = 0.8 - 0.6 * math.exp(-0.3 * layer)
            o = _diff_attention(q.reshape(shape), k.reshape(shape), v.reshape(shape), bias,
                                da_lq1[j], da_lk1[j], da_lq2[j], da_lk2[j], da_subln_g[j], lam_init)
            wo = da_wo[j]
        else:
            q, k, v = _qkv_proj(xf, sb_wq[j], sb_wk[j], sb_wv[j], -(SB_HEAD_DIM ** -0.5) * LOG2E)
            shape = (b, s, q.shape[1])
            o = _stick_breaking(q.reshape(shape), k.reshape(shape), v.reshape(shape))
            wo = sb_wo[j]
        x1, x1t, meta_i, meta_f, counts = _mix_ln_route(
            o.reshape(t, -1), xf, wo, ln_mix_g[layer], ln_mix_b[layer],
            moe_w_group[layer], moe_b_group[layer], moe_w_expert[layer], moe_b_expert[layer])
        xf = _hier_moe_ln(x1, x1t, meta_i, meta_f, counts, moe_w1, moe_w3, moe_w2, layer,
                          ln_ffn_g[layer], ln_ffn_b[layer])
    return xf.reshape(b, s, d).astype(x.dtype)
```

```python
import functools
import math

import jax
import jax.numpy as jnp
from jax import lax
from jax.experimental import pallas as pl
from jax.experimental.pallas import tpu as pltpu

F32 = jnp.float32
BF16 = jnp.bfloat16
I32 = jnp.int32

DEPTH = 2
DA_HEADS = 8
DA_HEAD_DIM = 64
SB_HEAD_DIM = 64
REL_BUCKETS = 32
REL_MAX_DIST = 128
MOE_GROUPS = 4
MOE_EXPERTS_PER_GROUP = 8
N_EXPERTS = MOE_GROUPS * MOE_EXPERTS_PER_GROUP
DEEPNORM_ALPHA = (2 * DEPTH) ** 0.25
LN_EPS = 1e-5
RMS_EPS = 1e-6
NEG_INF = -1e30
LOG2E = 1.4426950408889634
SB_DEAD_LOG2 = -160.0

LANES = 128
DA_BLOCK = 512
DA_PAR = 2
SB_BLOCK = 256
SB_PAR = 4
ROW_TILE = 512
DISPATCH_TILE = 1024
FFN_TILE = 256
META_ROWS = 8
ROW_GROUP = 8
VMEM_LIMIT = 48 * 1024 * 1024


def _params(n_axes, vmem=VMEM_LIMIT):
    return pltpu.CompilerParams(dimension_semantics=("arbitrary",) * n_axes, vmem_limit_bytes=vmem)


def _proj_kernel(x_ref, wq_ref, wk_ref, wv_ref, q_ref, k_ref, v_ref, *, q_scale):
    x = x_ref[...].astype(BF16)
    q = jnp.dot(x, wq_ref[...], preferred_element_type=F32)
    q_ref[...] = (q * q_scale).astype(BF16)
    k_ref[...] = jnp.dot(x, wk_ref[...], preferred_element_type=F32).astype(BF16)
    v_ref[...] = jnp.dot(x, wv_ref[...], preferred_element_type=F32).astype(BF16)


def _qkv_proj(x2d, wq, wk, wv, q_scale):
    t, d = x2d.shape
    n = wq.shape[1]
    tm = min(ROW_TILE, t)
    row_in = pl.BlockSpec((tm, d), lambda i: (i, 0))
    w_spec = pl.BlockSpec((d, n), lambda i: (0, 0))
    row_out = pl.BlockSpec((tm, n), lambda i: (i, 0))
    return pl.pallas_call(
        functools.partial(_proj_kernel, q_scale=q_scale),
        grid=(t // tm,),
        in_specs=[row_in, w_spec, w_spec, w_spec],
        out_specs=[row_out, row_out, row_out],
        out_shape=[jax.ShapeDtypeStruct((t, n), BF16)] * 3,
        compiler_params=_params(1),
        name="qkv_proj",
    )(x2d, wq.astype(BF16), wk.astype(BF16), wv.astype(BF16))


def _bias_kernel(tab_ref, out_ref, *, blk):
    h = pl.program_id(0)
    i = lax.broadcasted_iota(I32, (blk, 2 * blk), 0)
    j = lax.broadcasted_iota(I32, (blk, 2 * blk), 1)
    n = blk + i - j
    nn = jnp.maximum(n, 0)
    max_exact = REL_BUCKETS // 2
    nf = jnp.maximum(nn, 1).astype(F32)
    large = max_exact + (jnp.log(nf * (1.0 / max_exact)) * ((REL_BUCKETS - max_exact) / math.log(REL_MAX_DIST / max_exact))
                         ).astype(I32)
    large = jnp.minimum(large, REL_BUCKETS - 1)
    bucket = jnp.where(nn < max_exact, nn, large)
    far = tab_ref[REL_BUCKETS - 1, h]
    acc = jnp.zeros((blk, 2 * blk), F32)
    for b in range(REL_BUCKETS):
        acc = jnp.where(bucket == b, tab_ref[b, h] - far, acc)
    out_ref[0] = jnp.where(n >= 0, acc * LOG2E, NEG_INF)


def _bias_tiles(rel_table, blk):
    heads = rel_table.shape[1]
    return pl.pallas_call(
        functools.partial(_bias_kernel, blk=blk),
        grid=(heads,),
        in_specs=[pl.BlockSpec(memory_space=pltpu.SMEM)],
        out_specs=pl.BlockSpec((1, blk, 2 * blk), lambda h: (h, 0, 0)),
        out_shape=jax.ShapeDtypeStruct((heads, blk, 2 * blk), F32),
        compiler_params=_params(1),
        name="rel_bias_tiles",
    )(rel_table.astype(F32))


def _split_streams(q):
    lane = lax.broadcasted_iota(I32, q.shape, 1)
    zero = jnp.zeros_like(q)
    return jnp.concatenate([jnp.where(lane < 64, q, zero), jnp.where(lane >= 64, q, zero)], axis=0)


def _first_far_distance():
    max_exact = REL_BUCKETS // 2
    for n in range(max_exact, 16 * REL_MAX_DIST):
        if max_exact + int(math.log(n / max_exact) / math.log(REL_MAX_DIST / max_exact) * (REL_BUCKETS - max_exact)) \
                >= REL_BUCKETS - 1:
            return n
    raise ValueError("no distance reaches the last bucket")


NEAR_CORNER = -(-_first_far_distance() // LANES) * LANES


def _da_kernel(q_ref, k_ref, v_ref, bias_ref, lq1_ref, lk1_ref, lq2_ref, lk2_ref, g_ref, o_ref,
               m_sc, l_sc, acc_sc, s_sc, *, blk, lam_init, n_par, n_blocks):
    heads = range(n_par)
    lam = (jnp.exp(jnp.sum(lq1_ref[...] * lk1_ref[...], axis=1, keepdims=True))
           - jnp.exp(jnp.sum(lq2_ref[...] * lk2_ref[...], axis=1, keepdims=True)) + lam_init)

    def split_q(qi):
        start = pl.multiple_of(qi * blk, blk)
        return [_split_streams(q_ref[0, pl.ds(start, blk), c * LANES:(c + 1) * LANES]) for c in heads]

    def logits(qs, ki):
        start = pl.multiple_of(ki * blk, blk)
        return [lax.dot_general(qs[c], k_ref[0, pl.ds(start, blk), c * LANES:(c + 1) * LANES],
                                (((1,), (1,)), ((), ())), preferred_element_type=F32) for c in heads]

    def store_logits(vals):
        for c in heads:
            s_sc[c] = vals[c]

    def consume(ki, bias):
        start = pl.multiple_of(ki * blk, blk)
        alpha, p = [], []
        for c in heads:
            if bias == "near":
                corner = bias_ref[c, 0:NEAR_CORNER, blk - NEAR_CORNER:blk]
                for comp in range(2):
                    r0 = comp * blk
                    s_sc[c, r0:r0 + NEAR_CORNER, blk - NEAR_CORNER:blk] += corner
            sc = s_sc[c]
            if bias == "diagonal":
                tile = bias_ref[c, :, blk:2 * blk]
                sc = sc + jnp.concatenate([tile, tile], axis=0)
            m_prev = m_sc[c]
            m_next = jnp.maximum(m_prev, jnp.max(sc, axis=1, keepdims=True))
            m_sc[c] = m_next
            alpha.append(jnp.exp2(m_prev - m_next))
            pc = jnp.exp2(sc - jnp.concatenate([m_next] * (blk // LANES), axis=1))
            l_sc[c] = alpha[c] * l_sc[c] + sum(pc[:, n * LANES:(n + 1) * LANES] for n in range(blk // LANES))
            p.append(pc.astype(BF16))
        pv = [jnp.dot(p[c], v_ref[0, pl.ds(start, blk), c * LANES:(c + 1) * LANES],
                      preferred_element_type=F32) for c in heads]
        for c in heads:
            acc_sc[c] = alpha[c] * acc_sc[c] + pv[c]

    store_logits(logits(split_q(0), 0))

    def query_block(qi, carry):
        qs = split_q(qi)
        m_sc[...] = jnp.full(m_sc.shape, NEG_INF, F32)
        l_sc[...] = jnp.zeros(l_sc.shape, F32)
        acc_sc[...] = jnp.zeros(acc_sc.shape, F32)

        def step(ki, bias):
            nxt = logits(qs, ki + 1)
            consume(ki, bias)
            store_logits(nxt)

        def far_body(ki, c):
            step(ki, None)
            return c

        lax.fori_loop(0, jnp.maximum(qi - 1, 0), far_body, 0)

        @pl.when(qi >= 1)
        def _():
            step(qi - 1, "near")

        nxt = logits(split_q(jnp.minimum(qi + 1, n_blocks - 1)), 0)
        consume(qi, "diagonal")
        store_logits(nxt)

        rows = pl.ds(pl.multiple_of(qi * blk, blk), blk)
        for c in heads:
            inv_l = 1.0 / jnp.sum(l_sc[c], axis=1, keepdims=True)
            acc = acc_sc[c]
            o = acc[:blk] * inv_l[:blk] - lam * (acc[blk:] * inv_l[blk:])
            o = o * lax.rsqrt(jnp.mean(o * o, axis=1, keepdims=True) + RMS_EPS) * g_ref[...]
            o_ref[0, rows, c * LANES:(c + 1) * LANES] = (o * (1.0 - lam_init)).astype(BF16)
        return carry

    lax.fori_loop(0, n_blocks, query_block, 0)


def _diff_attention(q, k, v, bias, lq1, lk1, lq2, lk2, subln_g, lam_init):
    b, s, _ = q.shape
    blk, n_par = DA_BLOCK, DA_PAR
    assert NEAR_CORNER <= blk
    seq_spec = pl.BlockSpec((1, s, n_par * LANES), lambda bi, h: (bi, 0, h))
    vec64 = pl.BlockSpec((1, DA_HEAD_DIM), lambda bi, h: (0, 0))
    state = pltpu.VMEM((n_par, 2 * blk, LANES), F32)
    return pl.pallas_call(
        functools.partial(_da_kernel, blk=blk, lam_init=lam_init, n_par=n_par, n_blocks=s // blk),
        grid=(b, DA_HEADS // n_par),
        in_specs=[seq_spec, seq_spec, seq_spec,
                  pl.BlockSpec((n_par, blk, 2 * blk), lambda bi, h: (h, 0, 0)),
                  vec64, vec64, vec64, vec64,
                  pl.BlockSpec((1, LANES), lambda bi, h: (0, 0))],
        out_specs=seq_spec,
        out_shape=jax.ShapeDtypeStruct(q.shape, BF16),
        scratch_shapes=[state, state, state, pltpu.VMEM((n_par, 2 * blk, blk), F32)],
        compiler_params=_params(2),
        name="diff_attention",
    )(q, k, v, bias, lq1.reshape(1, -1), lk1.reshape(1, -1), lq2.reshape(1, -1), lk2.reshape(1, -1),
      subln_g.reshape(1, -1))


def _sb_kernel(q_ref, k_ref, v_ref, o_ref, r_sc, acc_sc, *, blk, n_par):
    qi = pl.program_id(2)
    rows = 2 * blk
    qs = [_split_streams(q_ref[0, :, c * LANES:(c + 1) * LANES]) for c in range(n_par)]
    jj = lax.broadcasted_iota(I32, (blk, blk), 0)
    ss = lax.broadcasted_iota(I32, (blk, blk), 1)
    later = jnp.where(jj > ss, 1.0, 0.0).astype(BF16)
    later2 = jnp.concatenate([later, later], axis=0)
    lane = lax.broadcasted_iota(I32, (blk, LANES), 1)
    r_sc[...] = jnp.zeros(r_sc.shape, F32)
    acc_sc[...] = jnp.zeros(acc_sc.shape, F32)

    def blocks(ki, diagonal):
        start = pl.multiple_of(ki * blk, blk)
        pairs = range(n_par)
        zn = [lax.dot_general(qs[c], k_ref[0, pl.ds(start, blk), c * LANES:(c + 1) * LANES],
                              (((1,), (1,)), ((), ())), preferred_element_type=F32) for c in pairs]
        if diagonal:
            r = lax.broadcasted_iota(I32, (rows, blk), 0)
            t = jnp.where(r >= blk, r - blk, r)
            strict = lax.broadcasted_iota(I32, (rows, blk), 1) < t
        log_fail, log_beta, hi_lo = [], [], []
        for c in pairs:
            neg_abs = pltpu.bitcast(pltpu.bitcast(zn[c], jnp.uint32) | jnp.uint32(0x80000000), F32)
            softplus = jnp.log(1.0 + jnp.exp2(neg_abs)) * LOG2E
            lf = jnp.minimum(zn[c], 0.0) - softplus
            log_beta.append(lf - zn[c])
            if diagonal:
                lf = jnp.where(strict, lf, 0.0)
            log_fail.append(lf)
            hi = pltpu.bitcast(pltpu.bitcast(lf, jnp.uint32) & jnp.uint32(0xFFFF0000), F32)
            hi_lo.append(jnp.concatenate([hi.astype(BF16), (lf - hi).astype(BF16)], axis=1))
        suffix = [jnp.dot(hi_lo[c], later2, preferred_element_type=F32) for c in pairs]
        wb = []
        for c in pairs:
            w = jnp.exp2(log_beta[c] + suffix[c])
            if diagonal:
                w = jnp.where(strict, w, 0.0)
            wb.append(w.astype(BF16))
        pv = []
        for c in pairs:
            vb = v_ref[0, pl.ds(start, blk), c * LANES:(c + 1) * LANES]
            pv.append((jnp.dot(wb[c][:blk], vb, preferred_element_type=F32),
                       jnp.dot(wb[c][blk:], vb, preferred_element_type=F32)))
        for c in pairs:
            carry = jnp.exp2(r_sc[c])
            acc_sc[c] += jnp.where(lane < 64, carry[:blk] * pv[c][0], carry[blk:] * pv[c][1])
            r_sc[c] += suffix[c][:, 0:1] + log_fail[c][:, 0:1]

    blocks(qi, True)

    def alive():
        return jnp.max(r_sc[...]) > SB_DEAD_LOG2

    def cond(carry):
        step, live = carry
        return jnp.logical_and(step < qi, live)

    def body(carry):
        step, _ = carry
        blocks(qi - 1 - step, False)
        return step + 1, alive()

    lax.while_loop(cond, body, (jnp.int32(0), alive()))
    for c in range(n_par):
        o_ref[0, :, c * LANES:(c + 1) * LANES] = acc_sc[c].astype(BF16)


def _stick_breaking(q, k, v):
    b, s, width = q.shape
    blk, n_par = SB_BLOCK, SB_PAR
    q_spec = pl.BlockSpec((1, blk, n_par * LANES), lambda bi, h, qi: (bi, qi, h))
    kv_spec = pl.BlockSpec((1, s, n_par * LANES), lambda bi, h, qi: (bi, 0, h))
    return pl.pallas_call(
        functools.partial(_sb_kernel, blk=blk, n_par=n_par),
        grid=(b, width // (n_par * LANES), s // blk),
        in_specs=[q_spec, kv_spec, kv_spec],
        out_specs=q_spec,
        out_shape=jax.ShapeDtypeStruct(q.shape, BF16),
        scratch_shapes=[pltpu.VMEM((n_par, 2 * blk, LANES), F32), pltpu.VMEM((n_par, blk, LANES), F32)],
        compiler_params=_params(3),
        name="stick_breaking",
    )(q, k, v)


def _layer_norm(y, g, b):
    mu = jnp.mean(y, axis=1, keepdims=True)
    yc = y - mu
    var = jnp.mean(yc * yc, axis=1, keepdims=True)
    return yc * lax.rsqrt(var + LN_EPS) * g + b


def _first_index_of_max(vals, lane_f):
    m = jnp.max(vals, axis=1, keepdims=True)
    idx = jnp.min(jnp.where(vals == m, lane_f, float(LANES)), axis=1, keepdims=True)
    return m, idx


def _top_experts(logits):
    lane_f = lax.broadcasted_iota(I32, logits.shape, 1).astype(F32)
    is_group = lane_f < MOE_GROUPS
    g_max, g_idx = _first_index_of_max(jnp.where(is_group, logits, NEG_INF), lane_f)
    g_gate = 1.0 / jnp.sum(jnp.where(is_group, jnp.exp(logits - g_max), 0.0), axis=1, keepdims=True)
    first = MOE_GROUPS + MOE_EXPERTS_PER_GROUP * g_idx
    in_group = (lane_f >= first) & (lane_f < first + MOE_EXPERTS_PER_GROUP)
    e_logits = jnp.where(in_group, logits, NEG_INF)
    m1, i1 = _first_index_of_max(e_logits, lane_f)
    m2, i2 = _first_index_of_max(jnp.where(lane_f == i1, NEG_INF, e_logits), lane_f)
    ratio = jnp.exp(m2 - m1)
    w1 = g_gate / (1.0 + ratio)
    return i1, i2, w1, w1 * ratio


def _mix_ln_route_kernel(o_ref, x_ref, wo_ref, g_ref, b_ref, wr_hi_ref, wr_lo_ref, br_ref,
                         x1_ref, x1t_ref, mi_ref, mf_ref, cnt_ref, *, tm):
    step = pl.program_id(0)

    @pl.when(step == 0)
    def _():
        cnt_ref[...] = jnp.zeros(cnt_ref.shape, F32)

    mix = jnp.dot(o_ref[...], wo_ref[...], preferred_element_type=F32)
    x1 = _layer_norm(DEEPNORM_ALPHA * x_ref[...] + mix, g_ref[...], b_ref[...])
    x1_ref[...] = x1
    _store_token_tiles(x1t_ref, x1)

    xh = x1.astype(BF16)
    xl = (x1 - xh.astype(F32)).astype(BF16)
    logits = (jnp.dot(xh, wr_hi_ref[...], preferred_element_type=F32)
              + jnp.dot(xh, wr_lo_ref[...], preferred_element_type=F32)
              + jnp.dot(xl, wr_hi_ref[...], preferred_element_type=F32)) + br_ref[...]
    i1, i2, w1, w2 = _top_experts(logits)

    lane = lax.broadcasted_iota(I32, (tm, LANES), 1)
    lane_f = lane.astype(F32)
    e1 = jnp.where(lane_f == i1, 1.0, 0.0)
    e2 = jnp.where(lane_f == i2, 1.0, 0.0)
    both = e1 + e2
    rr = lax.broadcasted_iota(I32, (tm, tm), 0)
    cc = lax.broadcasted_iota(I32, (tm, tm), 1)
    earlier = jnp.where(rr > cc, 1.0, 0.0).astype(BF16)
    before = cnt_ref[...] + jnp.dot(earlier, both.astype(BF16), preferred_element_type=F32)
    rank1 = jnp.sum(e1 * before, axis=1, keepdims=True)
    rank2 = jnp.sum(e2 * before, axis=1, keepdims=True)
    cnt_ref[...] += jnp.sum(both, axis=0, keepdims=True)

    meta = jnp.where(lane == 0, i1 - MOE_GROUPS,
                     jnp.where(lane == 1, i2 - MOE_GROUPS,
                               jnp.where(lane == 2, rank1, jnp.where(lane == 3, rank2, 0.0))))
    mi_ref[...] = meta.T[:META_ROWS].astype(I32)
    mf_ref[...] = jnp.where(lane == 0, w1, jnp.where(lane == 1, w2, 0.0))


def _mix_ln_route(o2d, x2d, wo, ln_g, ln_b, w_group, b_group, w_expert, b_expert):
    t, d = x2d.shape
    tm = min(ROW_TILE, t)
    pad = LANES - MOE_GROUPS - N_EXPERTS
    wr = jnp.pad(jnp.concatenate([w_group, w_expert], axis=1).astype(F32), ((0, 0), (0, pad)))
    wr_hi = wr.astype(BF16)
    wr_lo = (wr - wr_hi.astype(F32)).astype(BF16)
    br = jnp.pad(jnp.concatenate([b_group, b_expert]).astype(F32), (0, pad)).reshape(1, LANES)
    row = lambda n: pl.BlockSpec((tm, n), lambda i: (i, 0))
    full = lambda r, c: pl.BlockSpec((r, c), lambda i: (0, 0))
    return pl.pallas_call(
        functools.partial(_mix_ln_route_kernel, tm=tm),
        grid=(t // tm,),
        in_specs=[row(o2d.shape[1]), row(d), full(o2d.shape[1], d), full(1, d), full(1, d),
                  full(d, LANES), full(d, LANES), full(1, LANES)],
        out_specs=[row(d), pl.BlockSpec((tm * ROW_GROUP, LANES), lambda i: (i, 0)),
                   pl.BlockSpec((META_ROWS, tm), lambda i: (0, i)), row(LANES), full(1, LANES)],
        out_shape=[jax.ShapeDtypeStruct((t, d), F32), jax.ShapeDtypeStruct((t * ROW_GROUP, LANES), F32),
                   jax.ShapeDtypeStruct((META_ROWS, t), I32),
                   jax.ShapeDtypeStruct((t, LANES), F32), jax.ShapeDtypeStruct((1, LANES), F32)],
        compiler_params=_params(1),
        name="mix_ln_route",
    )(o2d, x2d, wo.astype(BF16), ln_g.reshape(1, d).astype(F32), ln_b.reshape(1, d).astype(F32),
      wr_hi, wr_lo, br)


def _load_token_tiles(ref, rows):
    return jnp.concatenate([ref[pl.ds(s, rows, stride=ROW_GROUP), :] for s in range(ROW_GROUP)], axis=1)


def _store_token_tiles(ref, val):
    for s in range(ROW_GROUP):
        ref[pl.ds(s, val.shape[0], stride=ROW_GROUP), :] = val[:, s * LANES:(s + 1) * LANES]


def _start_row_copies(tm, copy_of):
    def group(g, carry):
        for r in range(ROW_GROUP):
            for slot in range(2):
                copy_of(g * ROW_GROUP + r, slot).start(priority=slot)
        return carry

    lax.fori_loop(0, tm // ROW_GROUP, group, 0)


def _wait_row_copies(tm, copy_of):
    def group(g, carry):
        for _ in range(2 * ROW_GROUP):
            copy_of(0, 0).wait()
        return carry

    lax.fori_loop(0, tm // ROW_GROUP, group, 0)


def _row(ref, row):
    return ref.at[pl.ds(pl.multiple_of(row * ROW_GROUP, ROW_GROUP), ROW_GROUP)]


def _dispatch_kernel(pad_base_ref, pad_len_ref, tail_ref, pos_ref, x_ref, xs_ref, zeros_sc, sem, *, tm, tile):
    @pl.when(pl.program_id(0) == 0)
    def _():
        zeros_sc[...] = jnp.zeros(zeros_sc.shape, F32)

        def pad_rows(e, carry):
            def one(r, c, wait):
                cp = pltpu.make_async_copy(_row(zeros_sc, 0), _row(xs_ref, pad_base_ref[e] + r), sem)
                cp.wait() if wait else cp.start()
                return c
            lax.fori_loop(0, pad_len_ref[e], functools.partial(one, wait=False), 0)
            lax.fori_loop(0, pad_len_ref[e], functools.partial(one, wait=True), 0)
            return carry

        lax.fori_loop(0, N_EXPERTS, pad_rows, 0)

        def tail_tile(i, carry):
            n = tile * ROW_GROUP
            cp = pltpu.make_async_copy(zeros_sc, xs_ref.at[pl.ds(pl.multiple_of(i * n, n), n)], sem)
            cp.start()
            cp.wait()
            return carry

        lax.fori_loop(tail_ref[0], tail_ref[1], tail_tile, 0)

    copy_of = lambda row, slot: pltpu.make_async_copy(_row(x_ref, row), _row(xs_ref, pos_ref[slot * tm + row]), sem)
    _start_row_copies(tm, copy_of)
    _wait_row_copies(tm, copy_of)


def _pos_spec(tm):
    return pl.BlockSpec((2 * tm,), lambda i, *prefetch: (i,), memory_space=pltpu.SMEM)


def _token_tile_spec(rows, index_map):
    return pl.BlockSpec((rows * ROW_GROUP, LANES), index_map)


def _dispatch(plan, x1t, n_rows):
    t = x1t.shape[0] // ROW_GROUP
    tm = min(DISPATCH_TILE, t)
    grid_spec = pltpu.PrefetchScalarGridSpec(
        num_scalar_prefetch=3,
        grid=(t // tm,),
        in_specs=[_pos_spec(tm), _token_tile_spec(tm, lambda i, *prefetch: (i, 0))],
        out_specs=pl.BlockSpec(memory_space=pl.ANY),
        scratch_shapes=[pltpu.VMEM((FFN_TILE * ROW_GROUP, LANES), F32), pltpu.SemaphoreType.DMA(())],
    )
    return pl.pallas_call(
        functools.partial(_dispatch_kernel, tm=tm, tile=FFN_TILE),
        grid_spec=grid_spec,
        out_shape=jax.ShapeDtypeStruct((n_rows * ROW_GROUP, LANES), F32),
        compiler_params=_params(1),
        name="moe_dispatch",
    )(plan["pad_base"], plan["pad_len"], plan["tail"], plan["pos_dispatch"], x1t)


def _ffn_kernel(tile_expert_ref, n_used_ref, xs_ref, w1_ref, w3_ref, w2_ref, ys_ref, w1_sc, w3_sc, w2_sc):
    i = pl.program_id(0)
    used = i < n_used_ref[0]
    new_expert = jnp.logical_or(i == 0, tile_expert_ref[i] != tile_expert_ref[jnp.maximum(i - 1, 0)])

    @pl.when(jnp.logical_and(used, new_expert))
    def _():
        w1_sc[...] = w1_ref[0, 0].astype(BF16)
        w3_sc[...] = w3_ref[0, 0].astype(BF16)
        w2_sc[...] = w2_ref[0, 0].astype(BF16)

    @pl.when(used)
    def _():
        x = _load_token_tiles(xs_ref, xs_ref.shape[0] // ROW_GROUP).astype(BF16)
        a = jnp.dot(x, w1_sc[...], preferred_element_type=F32)
        b = jnp.dot(x, w3_sc[...], preferred_element_type=F32)
        h = (a * jax.nn.sigmoid(a) * b).astype(BF16)
        _store_token_tiles(ys_ref, jnp.dot(h, w2_sc[...], preferred_element_type=F32))

    @pl.when(jnp.logical_not(used))
    def _():
        ys_ref[...] = jnp.zeros(ys_ref.shape, F32)


def _expert_ffn(plan, xs, w1, w3, w2, layer):
    n_rows = xs.shape[0] // ROW_GROUP
    d, hidden = w1.shape[2], w1.shape[3]
    assert d == ROW_GROUP * LANES
    tm = FFN_TILE
    x_spec = _token_tile_spec(tm, lambda i, te, nu: (jnp.minimum(i, nu[0] - 1), 0))
    w_in = pl.BlockSpec((1, 1, d, hidden), lambda i, te, nu: (layer, te[i], 0, 0))
    w_out = pl.BlockSpec((1, 1, hidden, d), lambda i, te, nu: (layer, te[i], 0, 0))
    grid_spec = pltpu.PrefetchScalarGridSpec(
        num_scalar_prefetch=2,
        grid=(n_rows // tm,),
        in_specs=[x_spec, w_in, w_in, w_out],
        out_specs=_token_tile_spec(tm, lambda i, te, nu: (i, 0)),
        scratch_shapes=[pltpu.VMEM((d, hidden), BF16), pltpu.VMEM((d, hidden), BF16),
                        pltpu.VMEM((hidden, d), BF16)],
    )
    return pl.pallas_call(
        _ffn_kernel,
        grid_spec=grid_spec,
        out_shape=jax.ShapeDtypeStruct(xs.shape, F32),
        compiler_params=_params(1),
        name="moe_experts",
    )(plan["tile_expert"], plan["n_used"], xs, w1.astype(F32), w3.astype(F32), w2.astype(F32))


def _combine_kernel(pos_ref, pos_next_ref, x1_ref, mf_ref, ys_ref, g_ref, b_ref, out_ref, ybuf, sem, *, tm):
    i = pl.program_id(0)
    cur = lax.rem(i, 2)

    def gather(positions, buf):
        return lambda row, slot: pltpu.make_async_copy(_row(ys_ref, positions[slot * tm + row]),
                                                       _row(ybuf.at[buf, slot], row), sem.at[buf])

    @pl.when(i == 0)
    def _():
        _start_row_copies(tm, gather(pos_ref, cur))

    @pl.when(i + 1 < pl.num_programs(0))
    def _():
        _start_row_copies(tm, gather(pos_next_ref, 1 - cur))

    _wait_row_copies(tm, gather(pos_ref, cur))
    gates = mf_ref[...]
    y = (gates[:, 0:1] * _load_token_tiles(ybuf.at[cur, 0], tm)
         + gates[:, 1:2] * _load_token_tiles(ybuf.at[cur, 1], tm))
    out_ref[...] = _layer_norm(DEEPNORM_ALPHA * x1_ref[...] + y, g_ref[...], b_ref[...])


def _combine(plan, x1, meta_f, ys, ln_g, ln_b):
    t, d = x1.shape
    tm = min(ROW_TILE, t)
    row = lambda n: pl.BlockSpec((tm, n), lambda i: (i, 0))
    full = pl.BlockSpec((1, d), lambda i: (0, 0))
    last = t // tm - 1
    pos_next = pl.BlockSpec((2 * tm,), lambda i: (jnp.minimum(i + 1, last),), memory_space=pltpu.SMEM)
    return pl.pallas_call(
        functools.partial(_combine_kernel, tm=tm),
        grid=(t // tm,),
        in_specs=[_pos_spec(tm), pos_next, row(d), row(LANES), pl.BlockSpec(memory_space=pl.ANY), full, full],
        out_specs=row(d),
        out_shape=jax.ShapeDtypeStruct((t, d), F32),
        scratch_shapes=[pltpu.VMEM((2, 2, tm * ROW_GROUP, LANES), F32), pltpu.SemaphoreType.DMA((2,))],
        compiler_params=_params(1),
        name="moe_combine",
    )(plan["pos"], plan["pos"], x1, meta_f, ys, ln_g.reshape(1, d).astype(F32), ln_b.reshape(1, d).astype(F32))


def _routing_plan(meta_i, counts_row, n_tiles):
    counts = counts_row[0, MOE_GROUPS:MOE_GROUPS + N_EXPERTS].astype(I32)
    padded = ((counts + FFN_TILE - 1) // FFN_TILE) * FFN_TILE
    ends = jnp.cumsum(padded)
    offsets = ends - padded
    experts = jnp.arange(N_EXPERTS, dtype=I32)
    expert, rank = meta_i[0:2], meta_i[2:4]
    pos = rank + jnp.sum(jnp.where(expert[..., None] == experts, offsets, 0), axis=-1)
    tile_ends = ends // FFN_TILE
    tile_expert = jnp.sum(jnp.arange(n_tiles, dtype=I32)[:, None] >= tile_ends[None, :], axis=1)
    n_used = tile_ends[-1:].astype(I32)
    pos = pos.astype(I32)
    t = pos.shape[1]
    by_tile = lambda tm: pos.reshape(2, -1, tm).transpose(1, 0, 2).reshape(-1)
    return dict(pos=by_tile(min(ROW_TILE, t)), pos_dispatch=by_tile(min(DISPATCH_TILE, t)), tile_expert=jnp.minimum(tile_expert, N_EXPERTS - 1).astype(I32),
                n_used=n_used, pad_base=(offsets + counts).astype(I32), pad_len=(padded - counts).astype(I32),
                tail=jnp.concatenate([n_used, jnp.full((1,), n_tiles, I32)]))


def _hier_moe_ln(x1, x1t, meta_i, meta_f, counts_row, w1, w3, w2, layer, ln_g, ln_b):
    t, d = x1.shape
    n_tiles = (2 * t) // FFN_TILE + N_EXPERTS
    plan = _routing_plan(meta_i, counts_row, n_tiles)
    xs = _dispatch(plan, x1t, n_tiles * FFN_TILE)
    ys = _expert_ffn(plan, xs, w1, w3, w2, layer)
    return _combine(plan, x1, meta_f, ys, ln_g, ln_b)


def kernel(x, rel_table, da_wq, da_wk, da_wv, da_wo, da_lq1, da_lk1, da_lq2, da_lk2, da_subln_g, sb_wq, sb_wk,
           sb_wv, sb_wo, ln_mix_g, ln_mix_b, ln_ffn_g, ln_ffn_b, moe_w_group, moe_b_group, moe_w_expert,
           moe_b_expert, moe_w1, moe_w3, moe_w2):
    b, s, d = x.shape
    t = b * s
    assert s % DA_BLOCK == 0 and s % SB_BLOCK == 0 and (2 * t) % FFN_TILE == 0
    assert t % min(ROW_TILE, t) == 0 and t % min(DISPATCH_TILE, t) == 0
    xf = x.reshape(t, d).astype(F32)
    bias = _bias_tiles(rel_table, DA_BLOCK)
    for layer in range(DEPTH):
        j = layer // 2
        if layer % 2 == 0:
            q, k, v = _qkv_proj(xf, da_wq[j], da_wk[j], da_wv[j], DA_HEAD_DIM ** -0.5 * LOG2E)
            shape = (b, s, q.shape[1])
            lam_init = 0.8 - 0.6 * math.exp(-0.3 * layer)
            o = _diff_attention(q.reshape(shape), k.reshape(shape), v.reshape(shape), bias,
                                da_lq1[j], da_lk1[j], da_lq2[j], da_lk2[j], da_subln_g[j], lam_init)
            wo = da_wo[j]
        else:
            q, k, v = _qkv_proj(xf, sb_wq[j], sb_wk[j], sb_wv[j], -(SB_HEAD_DIM ** -0.5) * LOG2E)
            shape = (b, s, q.shape[1])
            o = _stick_breaking(q.reshape(shape), k.reshape(shape), v.reshape(shape))
            wo = sb_wo[j]
        x1, x1t, meta_i, meta_f, counts = _mix_ln_route(
            o.reshape(t, -1), xf, wo, ln_mix_g[layer], ln_mix_b[layer],
            moe_w_group[layer], moe_b_group[layer], moe_w_expert[layer], moe_b_expert[layer])
        xf = _hier_moe_ln(x1, x1t, meta_i, meta_f, counts, moe_w1, moe_w3, moe_w2, layer,
                          ln_ffn_g[layer], ln_ffn_b[layer])
    return xf.reshape(b, s, d).astype(x.dtype)
```

```python
import functools
import math

import jax
import jax.numpy as jnp
from jax import lax
from jax.experimental import pallas as pl
from jax.experimental.pallas import tpu as pltpu

F32 = jnp.float32
BF16 = jnp.bfloat16
I32 = jnp.int32

DEPTH = 2
DA_HEADS = 8
DA_HEAD_DIM = 64
SB_HEAD_DIM = 64
REL_BUCKETS = 32
REL_MAX_DIST = 128
MOE_GROUPS = 4
MOE_EXPERTS_PER_GROUP = 8
N_EXPERTS = MOE_GROUPS * MOE_EXPERTS_PER_GROUP
DEEPNORM_ALPHA = (2 * DEPTH) ** 0.25
LN_EPS = 1e-5
RMS_EPS = 1e-6
NEG_INF = -1e30
LOG2E = 1.4426950408889634
SB_DEAD_LOG2 = -160.0

LANES = 128
DA_BLOCK = 512
DA_PAR = 2
SB_BLOCK = 256
SB_PAR = 4
ROW_TILE = 512
DISPATCH_TILE = 1024
FFN_TILE = 256
META_ROWS = 8
ROW_GROUP = 8
VMEM_LIMIT = 48 * 1024 * 1024


def _params(n_axes, vmem=VMEM_LIMIT):
    return pltpu.CompilerParams(dimension_semantics=("arbitrary",) * n_axes, vmem_limit_bytes=vmem)


def _proj_kernel(x_ref, wq_ref, wk_ref, wv_ref, q_ref, k_ref, v_ref, *, q_scale):
    x = x_ref[...].astype(BF16)
    q = jnp.dot(x, wq_ref[...], preferred_element_type=F32)
    q_ref[...] = (q * q_scale).astype(BF16)
    k_ref[...] = jnp.dot(x, wk_ref[...], preferred_element_type=F32).astype(BF16)
    v_ref[...] = jnp.dot(x, wv_ref[...], preferred_element_type=F32).astype(BF16)


def _qkv_proj(x2d, wq, wk, wv, q_scale):
    t, d = x2d.shape
    n = wq.shape[1]
    tm = min(ROW_TILE, t)
    row_in = pl.BlockSpec((tm, d), lambda i: (i, 0))
    w_spec = pl.BlockSpec((d, n), lambda i: (0, 0))
    row_out = pl.BlockSpec((tm, n), lambda i: (i, 0))
    return pl.pallas_call(
        functools.partial(_proj_kernel, q_scale=q_scale),
        grid=(t // tm,),
        in_specs=[row_in, w_spec, w_spec, w_spec],
        out_specs=[row_out, row_out, row_out],
        out_shape=[jax.ShapeDtypeStruct((t, n), BF16)] * 3,
        compiler_params=_params(1),
        name="qkv_proj",
    )(x2d, wq.astype(BF16), wk.astype(BF16), wv.astype(BF16))


def _bias_kernel(tab_ref, out_ref, *, blk):
    h = pl.program_id(0)
    i = lax.broadcasted_iota(I32, (blk, 2 * blk), 0)
    j = lax.broadcasted_iota(I32, (blk, 2 * blk), 1)
    n = blk + i - j
    nn = jnp.maximum(n, 0)
    max_exact = REL_BUCKETS // 2
    nf = jnp.maximum(nn, 1).astype(F32)
    large = max_exact + (jnp.log(nf * (1.0 / max_exact)) * ((REL_BUCKETS - max_exact) / math.log(REL_MAX_DIST / max_exact))
                         ).astype(I32)
    large = jnp.minimum(large, REL_BUCKETS - 1)
    bucket = jnp.where(nn < max_exact, nn, large)
    far = tab_ref[REL_BUCKETS - 1, h]
    acc = jnp.zeros((blk, 2 * blk), F32)
    for b in range(REL_BUCKETS):
        acc = jnp.where(bucket == b, tab_ref[b, h] - far, acc)
    out_ref[0] = jnp.where(n >= 0, acc * LOG2E, NEG_INF)


def _bias_tiles(rel_table, blk):
    heads = rel_table.shape[1]
    return pl.pallas_call(
        functools.partial(_bias_kernel, blk=blk),
        grid=(heads,),
        in_specs=[pl.BlockSpec(memory_space=pltpu.SMEM)],
        out_specs=pl.BlockSpec((1, blk, 2 * blk), lambda h: (h, 0, 0)),
        out_shape=jax.ShapeDtypeStruct((heads, blk, 2 * blk), F32),
        compiler_params=_params(1),
        name="rel_bias_tiles",
    )(rel_table.astype(F32))


def _split_streams(q):
    lane = lax.broadcasted_iota(I32, q.shape, 1)
    zero = jnp.zeros_like(q)
    return jnp.concatenate([jnp.where(lane < 64, q, zero), jnp.where(lane >= 64, q, zero)], axis=0)


def _first_far_distance():
    max_exact = REL_BUCKETS // 2
    for n in range(max_exact, 16 * REL_MAX_DIST):
        if max_exact + int(math.log(n / max_exact) / math.log(REL_MAX_DIST / max_exact) * (REL_BUCKETS - max_exact)) \
                >= REL_BUCKETS - 1:
            return n
    raise ValueError("no distance reaches the last bucket")


NEAR_CORNER = -(-_first_far_distance() // LANES) * LANES


def _da_kernel(q_ref, k_ref, v_ref, bias_ref, lq1_ref, lk1_ref, lq2_ref, lk2_ref, g_ref, o_ref,
               m_sc, l_sc, acc_sc, s_sc, *, blk, lam_init, n_par, n_blocks):
    heads = range(n_par)
    lam = (jnp.exp(jnp.sum(lq1_ref[...] * lk1_ref[...], axis=1, keepdims=True))
           - jnp.exp(jnp.sum(lq2_ref[...] * lk2_ref[...], axis=1, keepdims=True)) + lam_init)

    def split_q(qi):
        start = pl.multiple_of(qi * blk, blk)
        return [_split_streams(q_ref[0, pl.ds(start, blk), c * LANES:(c + 1) * LANES]) for c in heads]

    def logits(qs, ki):
        start = pl.multiple_of(ki * blk, blk)
        return [lax.dot_general(qs[c], k_ref[0, pl.ds(start, blk), c * LANES:(c + 1) * LANES],
                                (((1,), (1,)), ((), ())), preferred_element_type=F32) for c in heads]

    def store_logits(vals):
        for c in heads:
            s_sc[c] = vals[c]

    def consume(ki, bias):
        start = pl.multiple_of(ki * blk, blk)
        alpha, p = [], []
        for c in heads:
            if bias == "near":
                corner = bias_ref[c, 0:NEAR_CORNER, blk - NEAR_CORNER:blk]
                for comp in range(2):
                    r0 = comp * blk
                    s_sc[c, r0:r0 + NEAR_CORNER, blk - NEAR_CORNER:blk] += corner
            sc = s_sc[c]
            if bias == "diagonal":
                tile = bias_ref[c, :, blk:2 * blk]
                sc = sc + jnp.concatenate([tile, tile], axis=0)
            m_prev = m_sc[c]
            m_next = jnp.maximum(m_prev, jnp.max(sc, axis=1, keepdims=True))
            m_sc[c] = m_next
            alpha.append(jnp.exp2(m_prev - m_next))
            pc = jnp.exp2(sc - jnp.concatenate([m_next] * (blk // LANES), axis=1))
            l_sc[c] = alpha[c] * l_sc[c] + sum(pc[:, n * LANES:(n + 1) * LANES] for n in range(blk // LANES))
            p.append(pc.astype(BF16))
        pv = [jnp.dot(p[c], v_ref[0, pl.ds(start, blk), c * LANES:(c + 1) * LANES],
                      preferred_element_type=F32) for c in heads]
        for c in heads:
            acc_sc[c] = alpha[c] * acc_sc[c] + pv[c]

    store_logits(logits(split_q(0), 0))

    def query_block(qi, carry):
        qs = split_q(qi)
        m_sc[...] = jnp.full(m_sc.shape, NEG_INF, F32)
        l_sc[...] = jnp.zeros(l_sc.shape, F32)
        acc_sc[...] = jnp.zeros(acc_sc.shape, F32)

        def step(ki, bias):
            nxt = logits(qs, ki + 1)
            consume(ki, bias)
            store_logits(nxt)

        def far_body(ki, c):
            step(ki, None)
            return c

        lax.fori_loop(0, jnp.maximum(qi - 1, 0), far_body, 0)

        @pl.when(qi >= 1)
        def _():
            step(qi - 1, "near")

        nxt = logits(split_q(jnp.minimum(qi + 1, n_blocks - 1)), 0)
        consume(qi, "diagonal")
        store_logits(nxt)

        rows = pl.ds(pl.multiple_of(qi * blk, blk), blk)
        for c in heads:
            inv_l = 1.0 / jnp.sum(l_sc[c], axis=1, keepdims=True)
            acc = acc_sc[c]
            o = acc[:blk] * inv_l[:blk] - lam * (acc[blk:] * inv_l[blk:])
            o = o * lax.rsqrt(jnp.mean(o * o, axis=1, keepdims=True) + RMS_EPS) * g_ref[...]
            o_ref[0, rows, c * LANES:(c + 1) * LANES] = (o * (1.0 - lam_init)).astype(BF16)
        return carry

    lax.fori_loop(0, n_blocks, query_block, 0)


def _diff_attention(q, k, v, bias, lq1, lk1, lq2, lk2, subln_g, lam_init):
    b, s, _ = q.shape
    blk, n_par = DA_BLOCK, DA_PAR
    assert NEAR_CORNER <= blk
    seq_spec = pl.BlockSpec((1, s, n_par * LANES), lambda bi, h: (bi, 0, h))
    vec64 = pl.BlockSpec((1, DA_HEAD_DIM), lambda bi, h: (0, 0))
    state = pltpu.VMEM((n_par, 2 * blk, LANES), F32)
    return pl.pallas_call(
        functools.partial(_da_kernel, blk=blk, lam_init=lam_init, n_par=n_par, n_blocks=s // blk),
        grid=(b, DA_HEADS // n_par),
        in_specs=[seq_spec, seq_spec, seq_spec,
                  pl.BlockSpec((n_par, blk, 2 * blk), lambda bi, h: (h, 0, 0)),
                  vec64, vec64, vec64, vec64,
                  pl.BlockSpec((1, LANES), lambda bi, h: (0, 0))],
        out_specs=seq_spec,
        out_shape=jax.ShapeDtypeStruct(q.shape, BF16),
        scratch_shapes=[state, state, state, pltpu.VMEM((n_par, 2 * blk, blk), F32)],
        compiler_params=_params(2),
        name="diff_attention",
    )(q, k, v, bias, lq1.reshape(1, -1), lk1.reshape(1, -1), lq2.reshape(1, -1), lk2.reshape(1, -1),
      subln_g.reshape(1, -1))


def _sb_kernel(q_ref, k_ref, v_ref, o_ref, later2_sc, r_sc, acc_sc, *, blk, n_par, n_blocks):
    rows = 2 * blk
    jj = lax.broadcasted_iota(I32, (blk, blk), 0)
    ss = lax.broadcasted_iota(I32, (blk, blk), 1)
    later = jnp.where(jj > ss, 1.0, 0.0).astype(BF16)
    later2_sc[...] = jnp.concatenate([later, later], axis=0)
    lane = lax.broadcasted_iota(I32, (blk, LANES), 1)
    lax.fori_loop(0, n_blocks, functools.partial(_sb_query_block, q_ref, k_ref, v_ref, o_ref, later2_sc, r_sc,
                                                 acc_sc, lane, blk, n_par, rows), 0)


def _sb_query_block(q_ref, k_ref, v_ref, o_ref, later2_sc, r_sc, acc_sc, lane, blk, n_par, rows, qi, carry):
    q_rows = pl.ds(pl.multiple_of(qi * blk, blk), blk)
    qs = [_split_streams(q_ref[0, q_rows, c * LANES:(c + 1) * LANES]) for c in range(n_par)]
    later2 = later2_sc[...]
    r_sc[...] = jnp.zeros(r_sc.shape, F32)
    acc_sc[...] = jnp.zeros(acc_sc.shape, F32)

    def blocks(ki, diagonal):
        start = pl.multiple_of(ki * blk, blk)
        pairs = range(n_par)
        zn = [lax.dot_general(qs[c], k_ref[0, pl.ds(start, blk), c * LANES:(c + 1) * LANES],
                              (((1,), (1,)), ((), ())), preferred_element_type=F32) for c in pairs]
        if diagonal:
            r = lax.broadcasted_iota(I32, (rows, blk), 0)
            t = jnp.where(r >= blk, r - blk, r)
            strict = lax.broadcasted_iota(I32, (rows, blk), 1) < t
        log_fail, log_beta, hi_lo = [], [], []
        for c in pairs:
            neg_abs = pltpu.bitcast(pltpu.bitcast(zn[c], jnp.uint32) | jnp.uint32(0x80000000), F32)
            softplus = jnp.log(1.0 + jnp.exp2(neg_abs)) * LOG2E
            lf = jnp.minimum(zn[c], 0.0) - softplus
            log_beta.append(lf - zn[c])
            if diagonal:
                lf = jnp.where(strict, lf, 0.0)
            log_fail.append(lf)
            hi = pltpu.bitcast(pltpu.bitcast(lf, jnp.uint32) & jnp.uint32(0xFFFF0000), F32)
            hi_lo.append(jnp.concatenate([hi.astype(BF16), (lf - hi).astype(BF16)], axis=1))
        suffix = [jnp.dot(hi_lo[c], later2, preferred_element_type=F32) for c in pairs]
        wb = []
        for c in pairs:
            w = jnp.exp2(log_beta[c] + suffix[c])
            if diagonal:
                w = jnp.where(strict, w, 0.0)
            wb.append(w.astype(BF16))
        pv = []
        for c in pairs:
            vb = v_ref[0, pl.ds(start, blk), c * LANES:(c + 1) * LANES]
            pv.append((jnp.dot(wb[c][:blk], vb, preferred_element_type=F32),
                       jnp.dot(wb[c][blk:], vb, preferred_element_type=F32)))
        for c in pairs:
            carry = jnp.exp2(r_sc[c])
            acc_sc[c] += jnp.where(lane < 64, carry[:blk] * pv[c][0], carry[blk:] * pv[c][1])
            r_sc[c] += suffix[c][:, 0:1] + log_fail[c][:, 0:1]

    blocks(qi, True)

    def alive():
        return jnp.max(r_sc[...]) > SB_DEAD_LOG2

    def cond(carry):
        step, live = carry
        return jnp.logical_and(step < qi, live)

    def body(carry):
        step, _ = carry
        blocks(qi - 1 - step, False)
        return step + 1, alive()

    lax.while_loop(cond, body, (jnp.int32(0), alive()))
    for c in range(n_par):
        o_ref[0, q_rows, c * LANES:(c + 1) * LANES] = acc_sc[c].astype(BF16)
    return carry


def _stick_breaking(q, k, v):
    b, s, width = q.shape
    blk, n_par = SB_BLOCK, SB_PAR
    seq_spec = pl.BlockSpec((1, s, n_par * LANES), lambda bi, h: (bi, 0, h))
    return pl.pallas_call(
        functools.partial(_sb_kernel, blk=blk, n_par=n_par, n_blocks=s // blk),
        grid=(b, width // (n_par * LANES)),
        in_specs=[seq_spec, seq_spec, seq_spec],
        out_specs=seq_spec,
        out_shape=jax.ShapeDtypeStruct(q.shape, BF16),
        scratch_shapes=[pltpu.VMEM((2 * blk, blk), BF16), pltpu.VMEM((n_par, 2 * blk, LANES), F32),
                        pltpu.VMEM((n_par, blk, LANES), F32)],
        compiler_params=_params(2),
        name="stick_breaking",
    )(q, k, v)


def _layer_norm(y, g, b):
    mu = jnp.mean(y, axis=1, keepdims=True)
    yc = y - mu
    var = jnp.mean(yc * yc, axis=1, keepdims=True)
    return yc * lax.rsqrt(var + LN_EPS) * g + b


def _first_index_of_max(vals, lane_f):
    m = jnp.max(vals, axis=1, keepdims=True)
    idx = jnp.min(jnp.where(vals == m, lane_f, float(LANES)), axis=1, keepdims=True)
    return m, idx


def _top_experts(logits):
    lane_f = lax.broadcasted_iota(I32, logits.shape, 1).astype(F32)
    is_group = lane_f < MOE_GROUPS
    g_max, g_idx = _first_index_of_max(jnp.where(is_group, logits, NEG_INF), lane_f)
    g_gate = 1.0 / jnp.sum(jnp.where(is_group, jnp.exp(logits - g_max), 0.0), axis=1, keepdims=True)
    first = MOE_GROUPS + MOE_EXPERTS_PER_GROUP * g_idx
    in_group = (lane_f >= first) & (lane_f < first + MOE_EXPERTS_PER_GROUP)
    e_logits = jnp.where(in_group, logits, NEG_INF)
    m1, i1 = _first_index_of_max(e_logits, lane_f)
    m2, i2 = _first_index_of_max(jnp.where(lane_f == i1, NEG_INF, e_logits), lane_f)
    ratio = jnp.exp(m2 - m1)
    w1 = g_gate / (1.0 + ratio)
    return i1, i2, w1, w1 * ratio


def _mix_ln_route_kernel(o_ref, x_ref, wo_ref, g_ref, b_ref, wr_hi_ref, wr_lo_ref, br_ref,
                         x1_ref, x1t_ref, mi_ref, mf_ref, cnt_ref, *, tm):
    step = pl.program_id(0)

    @pl.when(step == 0)
    def _():
        cnt_ref[...] = jnp.zeros(cnt_ref.shape, F32)

    mix = jnp.dot(o_ref[...], wo_ref[...], preferred_element_type=F32)
    x1 = _layer_norm(DEEPNORM_ALPHA * x_ref[...] + mix, g_ref[...], b_ref[...])
    x1_ref[...] = x1
    _store_token_tiles(x1t_ref, x1)

    xh = x1.astype(BF16)
    xl = (x1 - xh.astype(F32)).astype(BF16)
    logits = (jnp.dot(xh, wr_hi_ref[...], preferred_element_type=F32)
              + jnp.dot(xh, wr_lo_ref[...], preferred_element_type=F32)
              + jnp.dot(xl, wr_hi_ref[...], preferred_element_type=F32)) + br_ref[...]
    i1, i2, w1, w2 = _top_experts(logits)

    lane = lax.broadcasted_iota(I32, (tm, LANES), 1)
    lane_f = lane.astype(F32)
    e1 = jnp.where(lane_f == i1, 1.0, 0.0)
    e2 = jnp.where(lane_f == i2, 1.0, 0.0)
    both = e1 + e2
    rr = lax.broadcasted_iota(I32, (tm, tm), 0)
    cc = lax.broadcasted_iota(I32, (tm, tm), 1)
    earlier = jnp.where(rr > cc, 1.0, 0.0).astype(BF16)
    before = cnt_ref[...] + jnp.dot(earlier, both.astype(BF16), preferred_element_type=F32)
    rank1 = jnp.sum(e1 * before, axis=1, keepdims=True)
    rank2 = jnp.sum(e2 * before, axis=1, keepdims=True)
    cnt_ref[...] += jnp.sum(both, axis=0, keepdims=True)

    meta = jnp.where(lane == 0, i1 - MOE_GROUPS,
                     jnp.where(lane == 1, i2 - MOE_GROUPS,
                               jnp.where(lane == 2, rank1, jnp.where(lane == 3, rank2, 0.0))))
    mi_ref[...] = meta.T[:META_ROWS].astype(I32)
    mf_ref[...] = jnp.where(lane == 0, w1, jnp.where(lane == 1, w2, 0.0))


def _mix_ln_route(o2d, x2d, wo, ln_g, ln_b, w_group, b_group, w_expert, b_expert):
    t, d = x2d.shape
    tm = min(ROW_TILE, t)
    pad = LANES - MOE_GROUPS - N_EXPERTS
    wr = jnp.pad(jnp.concatenate([w_group, w_expert], axis=1).astype(F32), ((0, 0), (0, pad)))
    wr_hi = wr.astype(BF16)
    wr_lo = (wr - wr_hi.astype(F32)).astype(BF16)
    br = jnp.pad(jnp.concatenate([b_group, b_expert]).astype(F32), (0, pad)).reshape(1, LANES)
    row = lambda n: pl.BlockSpec((tm, n), lambda i: (i, 0))
    full = lambda r, c: pl.BlockSpec((r, c), lambda i: (0, 0))
    return pl.pallas_call(
        functools.partial(_mix_ln_route_kernel, tm=tm),
        grid=(t // tm,),
        in_specs=[row(o2d.shape[1]), row(d), full(o2d.shape[1], d), full(1, d), full(1, d),
                  full(d, LANES), full(d, LANES), full(1, LANES)],
        out_specs=[row(d), pl.BlockSpec((tm * ROW_GROUP, LANES), lambda i: (i, 0)),
                   pl.BlockSpec((META_ROWS, tm), lambda i: (0, i)), row(LANES), full(1, LANES)],
        out_shape=[jax.ShapeDtypeStruct((t, d), F32), jax.ShapeDtypeStruct((t * ROW_GROUP, LANES), F32),
                   jax.ShapeDtypeStruct((META_ROWS, t), I32),
                   jax.ShapeDtypeStruct((t, LANES), F32), jax.ShapeDtypeStruct((1, LANES), F32)],
        compiler_params=_params(1),
        name="mix_ln_route",
    )(o2d, x2d, wo.astype(BF16), ln_g.reshape(1, d).astype(F32), ln_b.reshape(1, d).astype(F32),
      wr_hi, wr_lo, br)


def _load_token_tiles(ref, rows):
    return jnp.concatenate([ref[pl.ds(s, rows, stride=ROW_GROUP), :] for s in range(ROW_GROUP)], axis=1)


def _store_token_tiles(ref, val):
    for s in range(ROW_GROUP):
        ref[pl.ds(s, val.shape[0], stride=ROW_GROUP), :] = val[:, s * LANES:(s + 1) * LANES]


def _start_row_copies(tm, copy_of):
    def group(g, carry):
        for r in range(ROW_GROUP):
            for slot in range(2):
                copy_of(g * ROW_GROUP + r, slot).start(priority=slot)
        return carry

    lax.fori_loop(0, tm // ROW_GROUP, group, 0)


def _wait_row_copies(tm, copy_of):
    def group(g, carry):
        for _ in range(2 * ROW_GROUP):
            copy_of(0, 0).wait()
        return carry

    lax.fori_loop(0, tm // ROW_GROUP, group, 0)


def _row(ref, row):
    return ref.at[pl.ds(pl.multiple_of(row * ROW_GROUP, ROW_GROUP), ROW_GROUP)]


def _dispatch_kernel(pad_base_ref, pad_len_ref, tail_ref, pos_ref, x_ref, xs_ref, zeros_sc, sem, *, tm, tile):
    @pl.when(pl.program_id(0) == 0)
    def _():
        zeros_sc[...] = jnp.zeros(zeros_sc.shape, F32)

        def pad_rows(e, carry):
            def one(r, c, wait):
                cp = pltpu.make_async_copy(_row(zeros_sc, 0), _row(xs_ref, pad_base_ref[e] + r), sem)
                cp.wait() if wait else cp.start()
                return c
            lax.fori_loop(0, pad_len_ref[e], functools.partial(one, wait=False), 0)
            lax.fori_loop(0, pad_len_ref[e], functools.partial(one, wait=True), 0)
            return carry

        lax.fori_loop(0, N_EXPERTS, pad_rows, 0)

        def tail_tile(i, carry):
            n = tile * ROW_GROUP
            cp = pltpu.make_async_copy(zeros_sc, xs_ref.at[pl.ds(pl.multiple_of(i * n, n), n)], sem)
            cp.start()
            cp.wait()
            return carry

        lax.fori_loop(tail_ref[0], tail_ref[1], tail_tile, 0)

    copy_of = lambda row, slot: pltpu.make_async_copy(_row(x_ref, row), _row(xs_ref, pos_ref[slot * tm + row]), sem)
    _start_row_copies(tm, copy_of)
    _wait_row_copies(tm, copy_of)


def _pos_spec(tm):
    return pl.BlockSpec((2 * tm,), lambda i, *prefetch: (i,), memory_space=pltpu.SMEM)


def _token_tile_spec(rows, index_map):
    return pl.BlockSpec((rows * ROW_GROUP, LANES), index_map)


def _dispatch(plan, x1t, n_rows):
    t = x1t.shape[0] // ROW_GROUP
    tm = min(DISPATCH_TILE, t)
    grid_spec = pltpu.PrefetchScalarGridSpec(
        num_scalar_prefetch=3,
        grid=(t // tm,),
        in_specs=[_pos_spec(tm), _token_tile_spec(tm, lambda i, *prefetch: (i, 0))],
        out_specs=pl.BlockSpec(memory_space=pl.ANY),
        scratch_shapes=[pltpu.VMEM((FFN_TILE * ROW_GROUP, LANES), F32), pltpu.SemaphoreType.DMA(())],
    )
    return pl.pallas_call(
        functools.partial(_dispatch_kernel, tm=tm, tile=FFN_TILE),
        grid_spec=grid_spec,
        out_shape=jax.ShapeDtypeStruct((n_rows * ROW_GROUP, LANES), F32),
        compiler_params=_params(1),
        name="moe_dispatch",
    )(plan["pad_base"], plan["pad_len"], plan["tail"], plan["pos_dispatch"], x1t)


def _ffn_kernel(tile_expert_ref, n_used_ref, xs_ref, w1_ref, w3_ref, w2_ref, ys_ref, w1_sc, w3_sc, w2_sc):
    i = pl.program_id(0)
    used = i < n_used_ref[0]
    new_expert = jnp.logical_or(i == 0, tile_expert_ref[i] != tile_expert_ref[jnp.maximum(i - 1, 0)])

    @pl.when(jnp.logical_and(used, new_expert))
    def _():
        w1_sc[...] = w1_ref[0, 0].astype(BF16)
        w3_sc[...] = w3_ref[0, 0].astype(BF16)
        w2_sc[...] = w2_ref[0, 0].astype(BF16)

    @pl.when(used)
    def _():
        x = _load_token_tiles(xs_ref, xs_ref.shape[0] // ROW_GROUP).astype(BF16)
        a = jnp.dot(x, w1_sc[...], preferred_element_type=F32)
        b = jnp.dot(x, w3_sc[...], preferred_element_type=F32)
        h = (a * jax.nn.sigmoid(a) * b).astype(BF16)
        _store_token_tiles(ys_ref, jnp.dot(h, w2_sc[...], preferred_element_type=F32))

    @pl.when(jnp.logical_not(used))
    def _():
        ys_ref[...] = jnp.zeros(ys_ref.shape, F32)


def _expert_ffn(plan, xs, w1, w3, w2, layer):
    n_rows = xs.shape[0] // ROW_GROUP
    d, hidden = w1.shape[2], w1.shape[3]
    assert d == ROW_GROUP * LANES
    tm = FFN_TILE
    x_spec = _token_tile_spec(tm, lambda i, te, nu: (jnp.minimum(i, nu[0] - 1), 0))
    w_in = pl.BlockSpec((1, 1, d, hidden), lambda i, te, nu: (layer, te[i], 0, 0))
    w_out = pl.BlockSpec((1, 1, hidden, d), lambda i, te, nu: (layer, te[i], 0, 0))
    grid_spec = pltpu.PrefetchScalarGridSpec(
        num_scalar_prefetch=2,
        grid=(n_rows // tm,),
        in_specs=[x_spec, w_in, w_in, w_out],
        out_specs=_token_tile_spec(tm, lambda i, te, nu: (i, 0)),
        scratch_shapes=[pltpu.VMEM((d, hidden), BF16), pltpu.VMEM((d, hidden), BF16),
                        pltpu.VMEM((hidden, d), BF16)],
    )
    return pl.pallas_call(
        _ffn_kernel,
        grid_spec=grid_spec,
        out_shape=jax.ShapeDtypeStruct(xs.shape, F32),
        compiler_params=_params(1),
        name="moe_experts",
    )(plan["tile_expert"], plan["n_used"], xs, w1.astype(F32), w3.astype(F32), w2.astype(F32))


def _combine_kernel(pos_ref, pos_next_ref, x1_ref, mf_ref, ys_ref, g_ref, b_ref, out_ref, ybuf, sem, *, tm):
    i = pl.program_id(0)
    cur = lax.rem(i, 2)

    def gather(positions, buf):
        return lambda row, slot: pltpu.make_async_copy(_row(ys_ref, positions[slot * tm + row]),
                                                       _row(ybuf.at[buf, slot], row), sem.at[buf])

    @pl.when(i == 0)
    def _():
        _start_row_copies(tm, gather(pos_ref, cur))

    @pl.when(i + 1 < pl.num_programs(0))
    def _():
        _start_row_copies(tm, gather(pos_next_ref, 1 - cur))

    _wait_row_copies(tm, gather(pos_ref, cur))
    gates = mf_ref[...]
    y = (gates[:, 0:1] * _load_token_tiles(ybuf.at[cur, 0], tm)
         + gates[:, 1:2] * _load_token_tiles(ybuf.at[cur, 1], tm))
    out_ref[...] = _layer_norm(DEEPNORM_ALPHA * x1_ref[...] + y, g_ref[...], b_ref[...])


def _combine(plan, x1, meta_f, ys, ln_g, ln_b):
    t, d = x1.shape
    tm = min(ROW_TILE, t)
    row = lambda n: pl.BlockSpec((tm, n), lambda i: (i, 0))
    full = pl.BlockSpec((1, d), lambda i: (0, 0))
    last = t // tm - 1
    pos_next = pl.BlockSpec((2 * tm,), lambda i: (jnp.minimum(i + 1, last),), memory_space=pltpu.SMEM)
    return pl.pallas_call(
        functools.partial(_combine_kernel, tm=tm),
        grid=(t // tm,),
        in_specs=[_pos_spec(tm), pos_next, row(d), row(LANES), pl.BlockSpec(memory_space=pl.ANY), full, full],
        out_specs=row(d),
        out_shape=jax.ShapeDtypeStruct((t, d), F32),
        scratch_shapes=[pltpu.VMEM((2, 2, tm * ROW_GROUP, LANES), F32), pltpu.SemaphoreType.DMA((2,))],
        compiler_params=_params(1),
        name="moe_combine",
    )(plan["pos"], plan["pos"], x1, meta_f, ys, ln_g.reshape(1, d).astype(F32), ln_b.reshape(1, d).astype(F32))


def _routing_plan(meta_i, counts_row, n_tiles):
    counts = counts_row[0, MOE_GROUPS:MOE_GROUPS + N_EXPERTS].astype(I32)
    padded = ((counts + FFN_TILE - 1) // FFN_TILE) * FFN_TILE
    ends = jnp.cumsum(padded)
    offsets = ends - padded
    experts = jnp.arange(N_EXPERTS, dtype=I32)
    expert, rank = meta_i[0:2], meta_i[2:4]
    pos = rank + jnp.sum(jnp.where(expert[..., None] == experts, offsets, 0), axis=-1)
    tile_ends = ends // FFN_TILE
    tile_expert = jnp.sum(jnp.arange(n_tiles, dtype=I32)[:, None] >= tile_ends[None, :], axis=1)
    n_used = tile_ends[-1:].astype(I32)
    pos = pos.astype(I32)
    t = pos.shape[1]
    by_tile = lambda tm: pos.reshape(2, -1, tm).transpose(1, 0, 2).reshape(-1)
    return dict(pos=by_tile(min(ROW_TILE, t)), pos_dispatch=by_tile(min(DISPATCH_TILE, t)), tile_expert=jnp.minimum(tile_expert, N_EXPERTS - 1).astype(I32),
                n_used=n_used, pad_base=(offsets + counts).astype(I32), pad_len=(padded - counts).astype(I32),
                tail=jnp.concatenate([n_used, jnp.full((1,), n_tiles, I32)]))


def _hier_moe_ln(x1, x1t, meta_i, meta_f, counts_row, w1, w3, w2, layer, ln_g, ln_b):
    t, d = x1.shape
    n_tiles = (2 * t) // FFN_TILE + N_EXPERTS
    plan = _routing_plan(meta_i, counts_row, n_tiles)
    xs = _dispatch(plan, x1t, n_tiles * FFN_TILE)
    ys = _expert_ffn(plan, xs, w1, w3, w2, layer)
    return _combine(plan, x1, meta_f, ys, ln_g, ln_b)


def kernel(x, rel_table, da_wq, da_wk, da_wv, da_wo, da_lq1, da_lk1, da_lq2, da_lk2, da_subln_g, sb_wq, sb_wk,
           sb_wv, sb_wo, ln_mix_g, ln_mix_b, ln_ffn_g, ln_ffn_b, moe_w_group, moe_b_group, moe_w_expert,
           moe_b_expert, moe_w1, moe_w3, moe_w2):
    b, s, d = x.shape
    t = b * s
    assert s % DA_BLOCK == 0 and s % SB_BLOCK == 0 and (2 * t) % FFN_TILE == 0
    assert t % min(ROW_TILE, t) == 0 and t % min(DISPATCH_TILE, t) == 0
    xf = x.reshape(t, d).astype(F32)
    bias = _bias_tiles(rel_table, DA_BLOCK)
    for layer in range(DEPTH):
        j = layer // 2
        if layer % 2 == 0:
            q, k, v = _qkv_proj(xf, da_wq[j], da_wk[j], da_wv[j], DA_HEAD_DIM ** -0.5 * LOG2E)
            shape = (b, s, q.shape[1])
            lam_init = 0.8 - 0.6 * math.exp(-0.3 * layer)
            o = _diff_attention(q.reshape(shape), k.reshape(shape), v.reshape(shape), bias,
                                da_lq1[j], da_lk1[j], da_lq2[j], da_lk2[j], da_subln_g[j], lam_init)
            wo = da_wo[j]
        else:
            q, k, v = _qkv_proj(xf, sb_wq[j], sb_wk[j], sb_wv[j], -(SB_HEAD_DIM ** -0.5) * LOG2E)
            shape = (b, s, q.shape[1])
            o = _stick_breaking(q.reshape(shape), k.reshape(shape), v.reshape(shape))
            wo = sb_wo[j]
        x1, x1t, meta_i, meta_f, counts = _mix_ln_route(
            o.reshape(t, -1), xf, wo, ln_mix_g[layer], ln_mix_b[layer],
            moe_w_group[layer], moe_b_group[layer], moe_w_expert[layer], moe_b_expert[layer])
        xf = _hier_moe_ln(x1, x1t, meta_i, meta_f, counts, moe_w1, moe_w3, moe_w2, layer,
                          ln_ffn_g[layer], ln_ffn_b[layer])
    return xf.reshape(b, s, d).astype(x.dtype)
```

```python
import functools
import math

import jax
import jax.numpy as jnp
from jax import lax
from jax.experimental import pallas as pl
from jax.experimental.pallas import tpu as pltpu

F32 = jnp.float32
BF16 = jnp.bfloat16
I32 = jnp.int32

DEPTH = 2
DA_HEADS = 8
DA_HEAD_DIM = 64
SB_HEAD_DIM = 64
REL_BUCKETS = 32
REL_MAX_DIST = 128
MOE_GROUPS = 4
MOE_EXPERTS_PER_GROUP = 8
N_EXPERTS = MOE_GROUPS * MOE_EXPERTS_PER_GROUP
DEEPNORM_ALPHA = (2 * DEPTH) ** 0.25
LN_EPS = 1e-5
RMS_EPS = 1e-6
NEG_INF = -1e30
LOG2E = 1.4426950408889634
SB_DEAD_LOG2 = -160.0

LANES = 128
DA_BLOCK = 512
DA_PAR = 2
SB_BLOCK = 256
SB_PAR = 4
ROW_TILE = 512
DISPATCH_TILE = 1024
FFN_TILE = 256
META_ROWS = 8
ROW_GROUP = 8
VMEM_LIMIT = 48 * 1024 * 1024


def _params(n_axes, vmem=VMEM_LIMIT):
    return pltpu.CompilerParams(dimension_semantics=("arbitrary",) * n_axes, vmem_limit_bytes=vmem)


def _proj_kernel(x_ref, wq_ref, wk_ref, wv_ref, q_ref, k_ref, v_ref, *, q_scale):
    x = x_ref[...].astype(BF16)
    q = jnp.dot(x, wq_ref[...], preferred_element_type=F32)
    q_ref[...] = (q * q_scale).astype(BF16)
    k_ref[...] = jnp.dot(x, wk_ref[...], preferred_element_type=F32).astype(BF16)
    v_ref[...] = jnp.dot(x, wv_ref[...], preferred_element_type=F32).astype(BF16)


def _qkv_proj(x2d, wq, wk, wv, q_scale):
    t, d = x2d.shape
    n = wq.shape[1]
    tm = min(ROW_TILE, t)
    row_in = pl.BlockSpec((tm, d), lambda i: (i, 0))
    w_spec = pl.BlockSpec((d, n), lambda i: (0, 0))
    row_out = pl.BlockSpec((tm, n), lambda i: (i, 0))
    return pl.pallas_call(
        functools.partial(_proj_kernel, q_scale=q_scale),
        grid=(t // tm,),
        in_specs=[row_in, w_spec, w_spec, w_spec],
        out_specs=[row_out, row_out, row_out],
        out_shape=[jax.ShapeDtypeStruct((t, n), BF16)] * 3,
        compiler_params=_params(1),
        name="qkv_proj",
    )(x2d, wq.astype(BF16), wk.astype(BF16), wv.astype(BF16))


def _bias_kernel(tab_ref, out_ref, *, blk):
    h = pl.program_id(0)
    max_exact = REL_BUCKETS // 2
    far = tab_ref[REL_BUCKETS - 1, h]
    far_distance = _first_far_distance()
    for a in range(blk // LANES):
        for b in range(2 * blk // LANES):
            rows, cols = slice(a * LANES, (a + 1) * LANES), slice(b * LANES, (b + 1) * LANES)
            n_low = blk + a * LANES - (b * LANES + LANES - 1)
            n_high = blk + a * LANES + LANES - 1 - b * LANES
            if n_low >= far_distance:
                out_ref[0, rows, cols] = jnp.zeros((LANES, LANES), F32)
                continue
            if n_high < 0:
                out_ref[0, rows, cols] = jnp.full((LANES, LANES), NEG_INF, F32)
                continue
            i = lax.broadcasted_iota(I32, (LANES, LANES), 0)
            j = lax.broadcasted_iota(I32, (LANES, LANES), 1)
            n = n_low + (LANES - 1) + i - j
            nn = jnp.maximum(n, 0)
            nf = jnp.maximum(nn, 1).astype(F32)
            large = max_exact + (jnp.log(nf * (1.0 / max_exact))
                                 * ((REL_BUCKETS - max_exact) / math.log(REL_MAX_DIST / max_exact))).astype(I32)
            large = jnp.minimum(large, REL_BUCKETS - 1)
            bucket = jnp.where(nn < max_exact, nn, large)
            acc = jnp.zeros((LANES, LANES), F32)
            for t in range(REL_BUCKETS):
                acc = jnp.where(bucket == t, tab_ref[t, h] - far, acc)
            out_ref[0, rows, cols] = jnp.where(n >= 0, acc * LOG2E, NEG_INF)


def _bias_tiles(rel_table, blk):
    heads = rel_table.shape[1]
    return pl.pallas_call(
        functools.partial(_bias_kernel, blk=blk),
        grid=(heads,),
        in_specs=[pl.BlockSpec(memory_space=pltpu.SMEM)],
        out_specs=pl.BlockSpec((1, blk, 2 * blk), lambda h: (h, 0, 0)),
        out_shape=jax.ShapeDtypeStruct((heads, blk, 2 * blk), F32),
        compiler_params=_params(1),
        name="rel_bias_tiles",
    )(rel_table.astype(F32))


def _split_streams(q):
    lane = lax.broadcasted_iota(I32, q.shape, 1)
    zero = jnp.zeros_like(q)
    return jnp.concatenate([jnp.where(lane < 64, q, zero), jnp.where(lane >= 64, q, zero)], axis=0)


def _first_far_distance():
    max_exact = REL_BUCKETS // 2
    for n in range(max_exact, 16 * REL_MAX_DIST):
        if max_exact + int(math.log(n / max_exact) / math.log(REL_MAX_DIST / max_exact) * (REL_BUCKETS - max_exact)) \
                >= REL_BUCKETS - 1:
            return n
    raise ValueError("no distance reaches the last bucket")


NEAR_CORNER = -(-_first_far_distance() // LANES) * LANES


def _da_kernel(q_ref, k_ref, v_ref, bias_ref, lq1_ref, lk1_ref, lq2_ref, lk2_ref, g_ref, o_ref,
               m_sc, l_sc, acc_sc, s_sc, *, blk, lam_init, n_par, n_blocks):
    heads = range(n_par)
    lam = (jnp.exp(jnp.sum(lq1_ref[...] * lk1_ref[...], axis=1, keepdims=True))
           - jnp.exp(jnp.sum(lq2_ref[...] * lk2_ref[...], axis=1, keepdims=True)) + lam_init)

    def split_q(qi):
        start = pl.multiple_of(qi * blk, blk)
        return [_split_streams(q_ref[0, pl.ds(start, blk), c * LANES:(c + 1) * LANES]) for c in heads]

    def logits(qs, ki):
        start = pl.multiple_of(ki * blk, blk)
        return [lax.dot_general(qs[c], k_ref[0, pl.ds(start, blk), c * LANES:(c + 1) * LANES],
                                (((1,), (1,)), ((), ())), preferred_element_type=F32) for c in heads]

    def store_logits(vals):
        for c in heads:
            s_sc[c] = vals[c]

    def consume(ki, bias):
        start = pl.multiple_of(ki * blk, blk)
        alpha, p = [], []
        for c in heads:
            if bias == "near":
                corner = bias_ref[c, 0:NEAR_CORNER, blk - NEAR_CORNER:blk]
                for comp in range(2):
                    r0 = comp * blk
                    s_sc[c, r0:r0 + NEAR_CORNER, blk - NEAR_CORNER:blk] += corner
            sc = s_sc[c]
            if bias == "diagonal":
                tile = bias_ref[c, :, blk:2 * blk]
                sc = sc + jnp.concatenate([tile, tile], axis=0)
            m_prev = m_sc[c]
            m_next = jnp.maximum(m_prev, jnp.max(sc, axis=1, keepdims=True))
            m_sc[c] = m_next
            alpha.append(jnp.exp2(m_prev - m_next))
            pc = jnp.exp2(sc - jnp.concatenate([m_next] * (blk // LANES), axis=1))
            l_sc[c] = alpha[c] * l_sc[c] + sum(pc[:, n * LANES:(n + 1) * LANES] for n in range(blk // LANES))
            p.append(pc.astype(BF16))
        pv = [jnp.dot(p[c], v_ref[0, pl.ds(start, blk), c * LANES:(c + 1) * LANES],
                      preferred_element_type=F32) for c in heads]
        for c in heads:
            acc_sc[c] = alpha[c] * acc_sc[c] + pv[c]

    store_logits(logits(split_q(0), 0))

    def query_block(qi, carry):
        qs = split_q(qi)
        m_sc[...] = jnp.full(m_sc.shape, NEG_INF, F32)
        l_sc[...] = jnp.zeros(l_sc.shape, F32)
        acc_sc[...] = jnp.zeros(acc_sc.shape, F32)

        def step(ki, bias):
            nxt = logits(qs, ki + 1)
            consume(ki, bias)
            store_logits(nxt)

        def far_body(ki, c):
            step(ki, None)
            return c

        lax.fori_loop(0, jnp.maximum(qi - 1, 0), far_body, 0)

        @pl.when(qi >= 1)
        def _():
            step(qi - 1, "near")

        nxt = logits(split_q(jnp.minimum(qi + 1, n_blocks - 1)), 0)
        consume(qi, "diagonal")
        store_logits(nxt)

        rows = pl.ds(pl.multiple_of(qi * blk, blk), blk)
        for c in heads:
            inv_l = 1.0 / jnp.sum(l_sc[c], axis=1, keepdims=True)
            acc = acc_sc[c]
            o = acc[:blk] * inv_l[:blk] - lam * (acc[blk:] * inv_l[blk:])
            o = o * lax.rsqrt(jnp.mean(o * o, axis=1, keepdims=True) + RMS_EPS) * g_ref[...]
            o_ref[0, rows, c * LANES:(c + 1) * LANES] = (o * (1.0 - lam_init)).astype(BF16)
        return carry

    lax.fori_loop(0, n_blocks, query_block, 0)


def _diff_attention(q, k, v, bias, lq1, lk1, lq2, lk2, subln_g, lam_init):
    b, s, _ = q.shape
    blk, n_par = DA_BLOCK, DA_PAR
    assert NEAR_CORNER <= blk
    seq_spec = pl.BlockSpec((1, s, n_par * LANES), lambda bi, h: (bi, 0, h))
    vec64 = pl.BlockSpec((1, DA_HEAD_DIM), lambda bi, h: (0, 0))
    state = pltpu.VMEM((n_par, 2 * blk, LANES), F32)
    return pl.pallas_call(
        functools.partial(_da_kernel, blk=blk, lam_init=lam_init, n_par=n_par, n_blocks=s // blk),
        grid=(b, DA_HEADS // n_par),
        in_specs=[seq_spec, seq_spec, seq_spec,
                  pl.BlockSpec((n_par, blk, 2 * blk), lambda bi, h: (h, 0, 0)),
                  vec64, vec64, vec64, vec64,
                  pl.BlockSpec((1, LANES), lambda bi, h: (0, 0))],
        out_specs=seq_spec,
        out_shape=jax.ShapeDtypeStruct(q.shape, BF16),
        scratch_shapes=[state, state, state, pltpu.VMEM((n_par, 2 * blk, blk), F32)],
        compiler_params=_params(2),
        name="diff_attention",
    )(q, k, v, bias, lq1.reshape(1, -1), lk1.reshape(1, -1), lq2.reshape(1, -1), lk2.reshape(1, -1),
      subln_g.reshape(1, -1))


def _sb_kernel(q_ref, k_ref, v_ref, o_ref, later2_sc, r_sc, acc_sc, *, blk, n_par, n_blocks):
    rows = 2 * blk
    jj = lax.broadcasted_iota(I32, (blk, blk), 0)
    ss = lax.broadcasted_iota(I32, (blk, blk), 1)
    later = jnp.where(jj > ss, 1.0, 0.0).astype(BF16)
    later2_sc[...] = jnp.concatenate([later, later], axis=0)
    lane = lax.broadcasted_iota(I32, (blk, LANES), 1)
    lax.fori_loop(0, n_blocks, functools.partial(_sb_query_block, q_ref, k_ref, v_ref, o_ref, later2_sc, r_sc,
                                                 acc_sc, lane, blk, n_par, rows), 0)


def _sb_query_block(q_ref, k_ref, v_ref, o_ref, later2_sc, r_sc, acc_sc, lane, blk, n_par, rows, qi, carry):
    q_rows = pl.ds(pl.multiple_of(qi * blk, blk), blk)
    qs = [_split_streams(q_ref[0, q_rows, c * LANES:(c + 1) * LANES]) for c in range(n_par)]
    later2 = later2_sc[...]
    r_sc[...] = jnp.zeros(r_sc.shape, F32)
    acc_sc[...] = jnp.zeros(acc_sc.shape, F32)

    def blocks(ki, diagonal):
        start = pl.multiple_of(ki * blk, blk)
        pairs = range(n_par)
        zn = [lax.dot_general(qs[c], k_ref[0, pl.ds(start, blk), c * LANES:(c + 1) * LANES],
                              (((1,), (1,)), ((), ())), preferred_element_type=F32) for c in pairs]
        if diagonal:
            r = lax.broadcasted_iota(I32, (rows, blk), 0)
            t = jnp.where(r >= blk, r - blk, r)
            strict = lax.broadcasted_iota(I32, (rows, blk), 1) < t
        log_fail, log_beta, hi_lo = [], [], []
        for c in pairs:
            neg_abs = pltpu.bitcast(pltpu.bitcast(zn[c], jnp.uint32) | jnp.uint32(0x80000000), F32)
            softplus = jnp.log(1.0 + jnp.exp2(neg_abs)) * LOG2E
            lf = jnp.minimum(zn[c], 0.0) - softplus
            log_beta.append(lf - zn[c])
            if diagonal:
                lf = jnp.where(strict, lf, 0.0)
            log_fail.append(lf)
            hi = pltpu.bitcast(pltpu.bitcast(lf, jnp.uint32) & jnp.uint32(0xFFFF0000), F32)
            hi_lo.append(jnp.concatenate([hi.astype(BF16), (lf - hi).astype(BF16)], axis=1))
        suffix = [jnp.dot(hi_lo[c], later2, preferred_element_type=F32) for c in pairs]
        wb = []
        for c in pairs:
            w = jnp.exp2(log_beta[c] + suffix[c])
            if diagonal:
                w = jnp.where(strict, w, 0.0)
            wb.append(w.astype(BF16))
        pv = []
        for c in pairs:
            vb = v_ref[0, pl.ds(start, blk), c * LANES:(c + 1) * LANES]
            pv.append((jnp.dot(wb[c][:blk], vb, preferred_element_type=F32),
                       jnp.dot(wb[c][blk:], vb, preferred_element_type=F32)))
        for c in pairs:
            carry = jnp.exp2(r_sc[c])
            acc_sc[c] += jnp.where(lane < 64, carry[:blk] * pv[c][0], carry[blk:] * pv[c][1])
            r_sc[c] += suffix[c][:, 0:1] + log_fail[c][:, 0:1]

    blocks(qi, True)

    def alive():
        return jnp.max(r_sc[...]) > SB_DEAD_LOG2

    def cond(carry):
        step, live = carry
        return jnp.logical_and(step < qi, live)

    def body(carry):
        step, _ = carry
        blocks(qi - 1 - step, False)
        return step + 1, alive()

    lax.while_loop(cond, body, (jnp.int32(0), alive()))
    for c in range(n_par):
        o_ref[0, q_rows, c * LANES:(c + 1) * LANES] = acc_sc[c].astype(BF16)
    return carry


def _stick_breaking(q, k, v):
    b, s, width = q.shape
    blk, n_par = SB_BLOCK, SB_PAR
    seq_spec = pl.BlockSpec((1, s, n_par * LANES), lambda bi, h: (bi, 0, h))
    return pl.pallas_call(
        functools.partial(_sb_kernel, blk=blk, n_par=n_par, n_blocks=s // blk),
        grid=(b, width // (n_par * LANES)),
        in_specs=[seq_spec, seq_spec, seq_spec],
        out_specs=seq_spec,
        out_shape=jax.ShapeDtypeStruct(q.shape, BF16),
        scratch_shapes=[pltpu.VMEM((2 * blk, blk), BF16), pltpu.VMEM((n_par, 2 * blk, LANES), F32),
                        pltpu.VMEM((n_par, blk, LANES), F32)],
        compiler_params=_params(2),
        name="stick_breaking",
    )(q, k, v)


def _layer_norm(y, g, b):
    mu = jnp.mean(y, axis=1, keepdims=True)
    yc = y - mu
    var = jnp.mean(yc * yc, axis=1, keepdims=True)
    return yc * lax.rsqrt(var + LN_EPS) * g + b


def _first_index_of_max(vals, lane_f):
    m = jnp.max(vals, axis=1, keepdims=True)
    idx = jnp.min(jnp.where(vals == m, lane_f, float(LANES)), axis=1, keepdims=True)
    return m, idx


def _top_experts(logits):
    lane_f = lax.broadcasted_iota(I32, logits.shape, 1).astype(F32)
    is_group = lane_f < MOE_GROUPS
    g_max, g_idx = _first_index_of_max(jnp.where(is_group, logits, NEG_INF), lane_f)
    g_gate = 1.0 / jnp.sum(jnp.where(is_group, jnp.exp(logits - g_max), 0.0), axis=1, keepdims=True)
    first = MOE_GROUPS + MOE_EXPERTS_PER_GROUP * g_idx
    in_group = (lane_f >= first) & (lane_f < first + MOE_EXPERTS_PER_GROUP)
    e_logits = jnp.where(in_group, logits, NEG_INF)
    m1, i1 = _first_index_of_max(e_logits, lane_f)
    m2, i2 = _first_index_of_max(jnp.where(lane_f == i1, NEG_INF, e_logits), lane_f)
    ratio = jnp.exp(m2 - m1)
    w1 = g_gate / (1.0 + ratio)
    return i1, i2, w1, w1 * ratio


def _mix_ln_route_kernel(o_ref, x_ref, wo_ref, g_ref, b_ref, wr_hi_ref, wr_lo_ref, br_ref,
                         x1_ref, x1t_ref, mi_ref, mf_ref, cnt_ref, *, tm):
    step = pl.program_id(0)

    @pl.when(step == 0)
    def _():
        cnt_ref[...] = jnp.zeros(cnt_ref.shape, F32)

    mix = jnp.dot(o_ref[...], wo_ref[...], preferred_element_type=F32)
    x1 = _layer_norm(DEEPNORM_ALPHA * x_ref[...] + mix, g_ref[...], b_ref[...])
    x1_ref[...] = x1
    _store_token_tiles(x1t_ref, x1)

    xh = x1.astype(BF16)
    xl = (x1 - xh.astype(F32)).astype(BF16)
    logits = (jnp.dot(xh, wr_hi_ref[...], preferred_element_type=F32)
              + jnp.dot(xh, wr_lo_ref[...], preferred_element_type=F32)
              + jnp.dot(xl, wr_hi_ref[...], preferred_element_type=F32)) + br_ref[...]
    i1, i2, w1, w2 = _top_experts(logits)

    lane = lax.broadcasted_iota(I32, (tm, LANES), 1)
    lane_f = lane.astype(F32)
    e1 = jnp.where(lane_f == i1, 1.0, 0.0)
    e2 = jnp.where(lane_f == i2, 1.0, 0.0)
    both = e1 + e2
    rr = lax.broadcasted_iota(I32, (tm, tm), 0)
    cc = lax.broadcasted_iota(I32, (tm, tm), 1)
    earlier = jnp.where(rr > cc, 1.0, 0.0).astype(BF16)
    before = cnt_ref[...] + jnp.dot(earlier, both.astype(BF16), preferred_element_type=F32)
    rank1 = jnp.sum(e1 * before, axis=1, keepdims=True)
    rank2 = jnp.sum(e2 * before, axis=1, keepdims=True)
    cnt_ref[...] += jnp.sum(both, axis=0, keepdims=True)

    meta = jnp.where(lane == 0, i1 - MOE_GROUPS,
                     jnp.where(lane == 1, i2 - MOE_GROUPS,
                               jnp.where(lane == 2, rank1, jnp.where(lane == 3, rank2, 0.0))))
    mi_ref[...] = meta.T[:META_ROWS].astype(I32)
    mf_ref[...] = jnp.where(lane == 0, w1, jnp.where(lane == 1, w2, 0.0))


def _mix_ln_route(o2d, x2d, wo, ln_g, ln_b, w_group, b_group, w_expert, b_expert):
    t, d = x2d.shape
    tm = min(ROW_TILE, t)
    pad = LANES - MOE_GROUPS - N_EXPERTS
    wr = jnp.pad(jnp.concatenate([w_group, w_expert], axis=1).astype(F32), ((0, 0), (0, pad)))
    wr_hi = wr.astype(BF16)
    wr_lo = (wr - wr_hi.astype(F32)).astype(BF16)
    br = jnp.pad(jnp.concatenate([b_group, b_expert]).astype(F32), (0, pad)).reshape(1, LANES)
    row = lambda n: pl.BlockSpec((tm, n), lambda i: (i, 0))
    full = lambda r, c: pl.BlockSpec((r, c), lambda i: (0, 0))
    return pl.pallas_call(
        functools.partial(_mix_ln_route_kernel, tm=tm),
        grid=(t // tm,),
        in_specs=[row(o2d.shape[1]), row(d), full(o2d.shape[1], d), full(1, d), full(1, d),
                  full(d, LANES), full(d, LANES), full(1, LANES)],
        out_specs=[row(d), pl.BlockSpec((tm * ROW_GROUP, LANES), lambda i: (i, 0)),
                   pl.BlockSpec((META_ROWS, tm), lambda i: (0, i)), row(LANES), full(1, LANES)],
        out_shape=[jax.ShapeDtypeStruct((t, d), F32), jax.ShapeDtypeStruct((t * ROW_GROUP, LANES), F32),
                   jax.ShapeDtypeStruct((META_ROWS, t), I32),
                   jax.ShapeDtypeStruct((t, LANES), F32), jax.ShapeDtypeStruct((1, LANES), F32)],
        compiler_params=_params(1),
        name="mix_ln_route",
    )(o2d, x2d, wo.astype(BF16), ln_g.reshape(1, d).astype(F32), ln_b.reshape(1, d).astype(F32),
      wr_hi, wr_lo, br)


def _load_token_tiles(ref, rows):
    return jnp.concatenate([ref[pl.ds(s, rows, stride=ROW_GROUP), :] for s in range(ROW_GROUP)], axis=1)


def _store_token_tiles(ref, val):
    for s in range(ROW_GROUP):
        ref[pl.ds(s, val.shape[0], stride=ROW_GROUP), :] = val[:, s * LANES:(s + 1) * LANES]


def _start_row_copies(tm, copy_of):
    def group(g, carry):
        for r in range(ROW_GROUP):
            for slot in range(2):
                copy_of(g * ROW_GROUP + r, slot).start(priority=slot)
        return carry

    lax.fori_loop(0, tm // ROW_GROUP, group, 0)


def _wait_row_copies(tm, copy_of):
    def group(g, carry):
        for _ in range(2 * ROW_GROUP):
            copy_of(0, 0).wait()
        return carry

    lax.fori_loop(0, tm // ROW_GROUP, group, 0)


def _row(ref, row):
    return ref.at[pl.ds(pl.multiple_of(row * ROW_GROUP, ROW_GROUP), ROW_GROUP)]


def _dispatch_kernel(pad_base_ref, pad_len_ref, tail_ref, pos_ref, x_ref, xs_ref, zeros_sc, sem, *, tm, tile):
    @pl.when(pl.program_id(0) == 0)
    def _():
        zeros_sc[...] = jnp.zeros(zeros_sc.shape, F32)

        def pad_rows(e, carry):
            def one(r, c, wait):
                cp = pltpu.make_async_copy(_row(zeros_sc, 0), _row(xs_ref, pad_base_ref[e] + r), sem)
                cp.wait() if wait else cp.start()
                return c
            lax.fori_loop(0, pad_len_ref[e], functools.partial(one, wait=False), 0)
            lax.fori_loop(0, pad_len_ref[e], functools.partial(one, wait=True), 0)
            return carry

        lax.fori_loop(0, N_EXPERTS, pad_rows, 0)

        def tail_tile(i, carry):
            n = tile * ROW_GROUP
            cp = pltpu.make_async_copy(zeros_sc, xs_ref.at[pl.ds(pl.multiple_of(i * n, n), n)], sem)
            cp.start()
            cp.wait()
            return carry

        lax.fori_loop(tail_ref[0], tail_ref[1], tail_tile, 0)

    copy_of = lambda row, slot: pltpu.make_async_copy(_row(x_ref, row), _row(xs_ref, pos_ref[slot * tm + row]), sem)
    _start_row_copies(tm, copy_of)
    _wait_row_copies(tm, copy_of)


def _pos_spec(tm):
    return pl.BlockSpec((2 * tm,), lambda i, *prefetch: (i,), memory_space=pltpu.SMEM)


def _token_tile_spec(rows, index_map):
    return pl.BlockSpec((rows * ROW_GROUP, LANES), index_map)


def _dispatch(plan, x1t, n_rows):
    t = x1t.shape[0] // ROW_GROUP
    tm = min(DISPATCH_TILE, t)
    grid_spec = pltpu.PrefetchScalarGridSpec(
        num_scalar_prefetch=3,
        grid=(t // tm,),
        in_specs=[_pos_spec(tm), _token_tile_spec(tm, lambda i, *prefetch: (i, 0))],
        out_specs=pl.BlockSpec(memory_space=pl.ANY),
        scratch_shapes=[pltpu.VMEM((FFN_TILE * ROW_GROUP, LANES), F32), pltpu.SemaphoreType.DMA(())],
    )
    return pl.pallas_call(
        functools.partial(_dispatch_kernel, tm=tm, tile=FFN_TILE),
        grid_spec=grid_spec,
        out_shape=jax.ShapeDtypeStruct((n_rows * ROW_GROUP, LANES), F32),
        compiler_params=_params(1),
        name="moe_dispatch",
    )(plan["pad_base"], plan["pad_len"], plan["tail"], plan["pos_dispatch"], x1t)


def _ffn_kernel(tile_expert_ref, n_used_ref, xs_ref, w1_ref, w3_ref, w2_ref, ys_ref, w1_sc, w3_sc, w2_sc):
    i = pl.program_id(0)
    used = i < n_used_ref[0]
    new_expert = jnp.logical_or(i == 0, tile_expert_ref[i] != tile_expert_ref[jnp.maximum(i - 1, 0)])

    @pl.when(jnp.logical_and(used, new_expert))
    def _():
        w1_sc[...] = w1_ref[0, 0].astype(BF16)
        w3_sc[...] = w3_ref[0, 0].astype(BF16)
        w2_sc[...] = w2_ref[0, 0].astype(BF16)

    @pl.when(used)
    def _():
        x = _load_token_tiles(xs_ref, xs_ref.shape[0] // ROW_GROUP).astype(BF16)
        a = jnp.dot(x, w1_sc[...], preferred_element_type=F32)
        b = jnp.dot(x, w3_sc[...], preferred_element_type=F32)
        h = (a * jax.nn.sigmoid(a) * b).astype(BF16)
        _store_token_tiles(ys_ref, jnp.dot(h, w2_sc[...], preferred_element_type=F32))

    @pl.when(jnp.logical_not(used))
    def _():
        ys_ref[...] = jnp.zeros(ys_ref.shape, F32)


def _expert_ffn(plan, xs, w1, w3, w2, layer):
    n_rows = xs.shape[0] // ROW_GROUP
    d, hidden = w1.shape[2], w1.shape[3]
    assert d == ROW_GROUP * LANES
    tm = FFN_TILE
    x_spec = _token_tile_spec(tm, lambda i, te, nu: (jnp.minimum(i, nu[0] - 1), 0))
    w_in = pl.BlockSpec((1, 1, d, hidden), lambda i, te, nu: (layer, te[i], 0, 0))
    w_out = pl.BlockSpec((1, 1, hidden, d), lambda i, te, nu: (layer, te[i], 0, 0))
    grid_spec = pltpu.PrefetchScalarGridSpec(
        num_scalar_prefetch=2,
        grid=(n_rows // tm,),
        in_specs=[x_spec, w_in, w_in, w_out],
        out_specs=_token_tile_spec(tm, lambda i, te, nu: (i, 0)),
        scratch_shapes=[pltpu.VMEM((d, hidden), BF16), pltpu.VMEM((d, hidden), BF16),
                        pltpu.VMEM((hidden, d), BF16)],
    )
    return pl.pallas_call(
        _ffn_kernel,
        grid_spec=grid_spec,
        out_shape=jax.ShapeDtypeStruct(xs.shape, F32),
        compiler_params=_params(1),
        name="moe_experts",
    )(plan["tile_expert"], plan["n_used"], xs, w1.astype(F32), w3.astype(F32), w2.astype(F32))


def _combine_kernel(pos_ref, pos_next_ref, x1_ref, mf_ref, ys_ref, g_ref, b_ref, out_ref, ybuf, sem, *, tm):
    i = pl.program_id(0)
    cur = lax.rem(i, 2)

    def gather(positions, buf):
        return lambda row, slot: pltpu.make_async_copy(_row(ys_ref, positions[slot * tm + row]),
                                                       _row(ybuf.at[buf, slot], row), sem.at[buf])

    @pl.when(i == 0)
    def _():
        _start_row_copies(tm, gather(pos_ref, cur))

    @pl.when(i + 1 < pl.num_programs(0))
    def _():
        _start_row_copies(tm, gather(pos_next_ref, 1 - cur))

    _wait_row_copies(tm, gather(pos_ref, cur))
    gates = mf_ref[...]
    y = (gates[:, 0:1] * _load_token_tiles(ybuf.at[cur, 0], tm)
         + gates[:, 1:2] * _load_token_tiles(ybuf.at[cur, 1], tm))
    out_ref[...] = _layer_norm(DEEPNORM_ALPHA * x1_ref[...] + y, g_ref[...], b_ref[...])


def _combine(plan, x1, meta_f, ys, ln_g, ln_b):
    t, d = x1.shape
    tm = min(ROW_TILE, t)
    row = lambda n: pl.BlockSpec((tm, n), lambda i: (i, 0))
    full = pl.BlockSpec((1, d), lambda i: (0, 0))
    last = t // tm - 1
    pos_next = pl.BlockSpec((2 * tm,), lambda i: (jnp.minimum(i + 1, last),), memory_space=pltpu.SMEM)
    return pl.pallas_call(
        functools.partial(_combine_kernel, tm=tm),
        grid=(t // tm,),
        in_specs=[_pos_spec(tm), pos_next, row(d), row(LANES), pl.BlockSpec(memory_space=pl.ANY), full, full],
        out_specs=row(d),
        out_shape=jax.ShapeDtypeStruct((t, d), F32),
        scratch_shapes=[pltpu.VMEM((2, 2, tm * ROW_GROUP, LANES), F32), pltpu.SemaphoreType.DMA((2,))],
        compiler_params=_params(1),
        name="moe_combine",
    )(plan["pos"], plan["pos"], x1, meta_f, ys, ln_g.reshape(1, d).astype(F32), ln_b.reshape(1, d).astype(F32))


def _routing_plan(meta_i, counts_row, n_tiles):
    counts = counts_row[0, MOE_GROUPS:MOE_GROUPS + N_EXPERTS].astype(I32)
    padded = ((counts + FFN_TILE - 1) // FFN_TILE) * FFN_TILE
    ends = jnp.cumsum(padded)
    offsets = ends - padded
    experts = jnp.arange(N_EXPERTS, dtype=I32)
    expert, rank = meta_i[0:2], meta_i[2:4]
    pos = rank + jnp.sum(jnp.where(expert[..., None] == experts, offsets, 0), axis=-1)
    tile_ends = ends // FFN_TILE
    tile_expert = jnp.sum(jnp.arange(n_tiles, dtype=I32)[:, None] >= tile_ends[None, :], axis=1)
    n_used = tile_ends[-1:].astype(I32)
    pos = pos.astype(I32)
    t = pos.shape[1]
    by_tile = lambda tm: pos.reshape(2, -1, tm).transpose(1, 0, 2).reshape(-1)
    return dict(pos=by_tile(min(ROW_TILE, t)), pos_dispatch=by_tile(min(DISPATCH_TILE, t)), tile_expert=jnp.minimum(tile_expert, N_EXPERTS - 1).astype(I32),
                n_used=n_used, pad_base=(offsets + counts).astype(I32), pad_len=(padded - counts).astype(I32),
                tail=jnp.concatenate([n_used, jnp.full((1,), n_tiles, I32)]))


def _hier_moe_ln(x1, x1t, meta_i, meta_f, counts_row, w1, w3, w2, layer, ln_g, ln_b):
    t, d = x1.shape
    n_tiles = (2 * t) // FFN_TILE + N_EXPERTS
    plan = _routing_plan(meta_i, counts_row, n_tiles)
    xs = _dispatch(plan, x1t, n_tiles * FFN_TILE)
    ys = _expert_ffn(plan, xs, w1, w3, w2, layer)
    return _combine(plan, x1, meta_f, ys, ln_g, ln_b)


def kernel(x, rel_table, da_wq, da_wk, da_wv, da_wo, da_lq1, da_lk1, da_lq2, da_lk2, da_subln_g, sb_wq, sb_wk,
           sb_wv, sb_wo, ln_mix_g, ln_mix_b, ln_ffn_g, ln_ffn_b, moe_w_group, moe_b_group, moe_w_expert,
           moe_b_expert, moe_w1, moe_w3, moe_w2):
    b, s, d = x.shape
    t = b * s
    assert s % DA_BLOCK == 0 and s % SB_BLOCK == 0 and (2 * t) % FFN_TILE == 0
    assert t % min(ROW_TILE, t) == 0 and t % min(DISPATCH_TILE, t) == 0
    xf = x.reshape(t, d).astype(F32)
    bias = _bias_tiles(rel_table, DA_BLOCK)
    for layer in range(DEPTH):
        j = layer // 2
        if layer % 2 == 0:
            q, k, v = _qkv_proj(xf, da_wq[j], da_wk[j], da_wv[j], DA_HEAD_DIM ** -0.5 * LOG2E)
            shape = (b, s, q.shape[1])
            lam_init = 0.8 - 0.6 * math.exp(-0.3 * layer)
            o = _diff_attention(q.reshape(shape), k.reshape(shape), v.reshape(shape), bias,
                                da_lq1[j], da_lk1[j], da_lq2[j], da_lk2[j], da_subln_g[j], lam_init)
            wo = da_wo[j]
        else:
            q, k, v = _qkv_proj(xf, sb_wq[j], sb_wk[j], sb_wv[j], -(SB_HEAD_DIM ** -0.5) * LOG2E)
            shape = (b, s, q.shape[1])
            o = _stick_breaking(q.reshape(shape), k.reshape(shape), v.reshape(shape))
            wo = sb_wo[j]
        x1, x1t, meta_i, meta_f, counts = _mix_ln_route(
            o.reshape(t, -1), xf, wo, ln_mix_g[layer], ln_mix_b[layer],
            moe_w_group[layer], moe_b_group[layer], moe_w_expert[layer], moe_b_expert[layer])
        xf = _hier_moe_ln(x1, x1t, meta_i, meta_f, counts, moe_w1, moe_w3, moe_w2, layer,
                          ln_ffn_g[layer], ln_ffn_b[layer])
    return xf.reshape(b, s, d).astype(x.dtype)
```

```python
import functools
import math

import jax
import jax.numpy as jnp
from jax import lax
from jax.experimental import pallas as pl
from jax.experimental.pallas import tpu as pltpu

F32 = jnp.float32
BF16 = jnp.bfloat16
I32 = jnp.int32

DEPTH = 2
DA_HEADS = 8
DA_HEAD_DIM = 64
SB_HEAD_DIM = 64
REL_BUCKETS = 32
REL_MAX_DIST = 128
MOE_GROUPS = 4
MOE_EXPERTS_PER_GROUP = 8
N_EXPERTS = MOE_GROUPS * MOE_EXPERTS_PER_GROUP
DEEPNORM_ALPHA = (2 * DEPTH) ** 0.25
LN_EPS = 1e-5
RMS_EPS = 1e-6
NEG_INF = -1e30
LOG2E = 1.4426950408889634
SB_DEAD_LOG2 = -160.0

LANES = 128
DA_BLOCK = 512
DA_PAR = 2
SB_BLOCK = 256
SB_PAR = 4
ROW_TILE = 1024
DISPATCH_TILE = 1024
FFN_TILE = 256
META_ROWS = 8
ROW_GROUP = 8
VMEM_LIMIT = 48 * 1024 * 1024


def _params(n_axes, vmem=VMEM_LIMIT):
    return pltpu.CompilerParams(dimension_semantics=("arbitrary",) * n_axes, vmem_limit_bytes=vmem)


def _proj_kernel(x_ref, wq_ref, wk_ref, wv_ref, q_ref, k_ref, v_ref, *, q_scale):
    x = x_ref[...].astype(BF16)
    q = jnp.dot(x, wq_ref[...], preferred_element_type=F32)
    q_ref[...] = (q * q_scale).astype(BF16)
    k_ref[...] = jnp.dot(x, wk_ref[...], preferred_element_type=F32).astype(BF16)
    v_ref[...] = jnp.dot(x, wv_ref[...], preferred_element_type=F32).astype(BF16)


def _qkv_proj(x2d, wq, wk, wv, q_scale):
    t, d = x2d.shape
    n = wq.shape[1]
    tm = min(ROW_TILE, t)
    row_in = pl.BlockSpec((tm, d), lambda i: (i, 0))
    w_spec = pl.BlockSpec((d, n), lambda i: (0, 0))
    row_out = pl.BlockSpec((tm, n), lambda i: (i, 0))
    return pl.pallas_call(
        functools.partial(_proj_kernel, q_scale=q_scale),
        grid=(t // tm,),
        in_specs=[row_in, w_spec, w_spec, w_spec],
        out_specs=[row_out, row_out, row_out],
        out_shape=[jax.ShapeDtypeStruct((t, n), BF16)] * 3,
        compiler_params=_params(1),
        name="qkv_proj",
    )(x2d, wq.astype(BF16), wk.astype(BF16), wv.astype(BF16))


def _bias_kernel(tab_ref, out_ref, *, blk):
    h = pl.program_id(0)
    max_exact = REL_BUCKETS // 2
    far = tab_ref[REL_BUCKETS - 1, h]
    far_distance = _first_far_distance()
    for a in range(blk // LANES):
        for b in range(2 * blk // LANES):
            rows, cols = slice(a * LANES, (a + 1) * LANES), slice(b * LANES, (b + 1) * LANES)
            n_low = blk + a * LANES - (b * LANES + LANES - 1)
            n_high = blk + a * LANES + LANES - 1 - b * LANES
            if n_low >= far_distance:
                out_ref[0, rows, cols] = jnp.zeros((LANES, LANES), F32)
                continue
            if n_high < 0:
                out_ref[0, rows, cols] = jnp.full((LANES, LANES), NEG_INF, F32)
                continue
            i = lax.broadcasted_iota(I32, (LANES, LANES), 0)
            j = lax.broadcasted_iota(I32, (LANES, LANES), 1)
            n = n_low + (LANES - 1) + i - j
            nn = jnp.maximum(n, 0)
            nf = jnp.maximum(nn, 1).astype(F32)
            large = max_exact + (jnp.log(nf * (1.0 / max_exact))
                                 * ((REL_BUCKETS - max_exact) / math.log(REL_MAX_DIST / max_exact))).astype(I32)
            large = jnp.minimum(large, REL_BUCKETS - 1)
            bucket = jnp.where(nn < max_exact, nn, large)
            acc = jnp.zeros((LANES, LANES), F32)
            for t in range(REL_BUCKETS):
                acc = jnp.where(bucket == t, tab_ref[t, h] - far, acc)
            out_ref[0, rows, cols] = jnp.where(n >= 0, acc * LOG2E, NEG_INF)


def _bias_tiles(rel_table, blk):
    heads = rel_table.shape[1]
    return pl.pallas_call(
        functools.partial(_bias_kernel, blk=blk),
        grid=(heads,),
        in_specs=[pl.BlockSpec(memory_space=pltpu.SMEM)],
        out_specs=pl.BlockSpec((1, blk, 2 * blk), lambda h: (h, 0, 0)),
        out_shape=jax.ShapeDtypeStruct((heads, blk, 2 * blk), F32),
        compiler_params=_params(1),
        name="rel_bias_tiles",
    )(rel_table.astype(F32))


def _split_streams(q):
    lane = lax.broadcasted_iota(I32, q.shape, 1)
    zero = jnp.zeros_like(q)
    return jnp.concatenate([jnp.where(lane < 64, q, zero), jnp.where(lane >= 64, q, zero)], axis=0)


def _first_far_distance():
    max_exact = REL_BUCKETS // 2
    for n in range(max_exact, 16 * REL_MAX_DIST):
        if max_exact + int(math.log(n / max_exact) / math.log(REL_MAX_DIST / max_exact) * (REL_BUCKETS - max_exact)) \
                >= REL_BUCKETS - 1:
            return n
    raise ValueError("no distance reaches the last bucket")


NEAR_CORNER = -(-_first_far_distance() // LANES) * LANES


def _da_kernel(q_ref, k_ref, v_ref, bias_ref, lq1_ref, lk1_ref, lq2_ref, lk2_ref, g_ref, o_ref,
               m_sc, l_sc, acc_sc, s_sc, *, blk, lam_init, n_par, n_blocks):
    heads = range(n_par)
    lam = (jnp.exp(jnp.sum(lq1_ref[...] * lk1_ref[...], axis=1, keepdims=True))
           - jnp.exp(jnp.sum(lq2_ref[...] * lk2_ref[...], axis=1, keepdims=True)) + lam_init)

    def split_q(qi):
        start = pl.multiple_of(qi * blk, blk)
        return [_split_streams(q_ref[0, pl.ds(start, blk), c * LANES:(c + 1) * LANES]) for c in heads]

    def logits(qs, ki):
        start = pl.multiple_of(ki * blk, blk)
        return [lax.dot_general(qs[c], k_ref[0, pl.ds(start, blk), c * LANES:(c + 1) * LANES],
                                (((1,), (1,)), ((), ())), preferred_element_type=F32) for c in heads]

    def store_logits(vals):
        for c in heads:
            s_sc[c] = vals[c]

    def consume(ki, bias):
        start = pl.multiple_of(ki * blk, blk)
        alpha, p = [], []
        for c in heads:
            if bias == "near":
                corner = bias_ref[c, 0:NEAR_CORNER, blk - NEAR_CORNER:blk]
                for comp in range(2):
                    r0 = comp * blk
                    s_sc[c, r0:r0 + NEAR_CORNER, blk - NEAR_CORNER:blk] += corner
            sc = s_sc[c]
            if bias == "diagonal":
                tile = bias_ref[c, :, blk:2 * blk]
                sc = sc + jnp.concatenate([tile, tile], axis=0)
            m_prev = m_sc[c]
            m_next = jnp.maximum(m_prev, jnp.max(sc, axis=1, keepdims=True))
            m_sc[c] = m_next
            alpha.append(jnp.exp2(m_prev - m_next))
            pc = jnp.exp2(sc - jnp.concatenate([m_next] * (blk // LANES), axis=1))
            l_sc[c] = alpha[c] * l_sc[c] + sum(pc[:, n * LANES:(n + 1) * LANES] for n in range(blk // LANES))
            p.append(pc.astype(BF16))
        pv = [jnp.dot(p[c], v_ref[0, pl.ds(start, blk), c * LANES:(c + 1) * LANES],
                      preferred_element_type=F32) for c in heads]
        for c in heads:
            acc_sc[c] = alpha[c] * acc_sc[c] + pv[c]

    store_logits(logits(split_q(0), 0))

    def query_block(qi, carry):
        qs = split_q(qi)
        m_sc[...] = jnp.full(m_sc.shape, NEG_INF, F32)
        l_sc[...] = jnp.zeros(l_sc.shape, F32)
        acc_sc[...] = jnp.zeros(acc_sc.shape, F32)

        def step(ki, bias):
            nxt = logits(qs, ki + 1)
            consume(ki, bias)
            store_logits(nxt)

        def far_body(ki, c):
            step(ki, None)
            return c

        lax.fori_loop(0, jnp.maximum(qi - 1, 0), far_body, 0)

        @pl.when(qi >= 1)
        def _():
            step(qi - 1, "near")

        nxt = logits(split_q(jnp.minimum(qi + 1, n_blocks - 1)), 0)
        consume(qi, "diagonal")
        store_logits(nxt)

        rows = pl.ds(pl.multiple_of(qi * blk, blk), blk)
        for c in heads:
            inv_l = 1.0 / jnp.sum(l_sc[c], axis=1, keepdims=True)
            acc = acc_sc[c]
            o = acc[:blk] * inv_l[:blk] - lam * (acc[blk:] * inv_l[blk:])
            o = o * lax.rsqrt(jnp.mean(o * o, axis=1, keepdims=True) + RMS_EPS) * g_ref[...]
            o_ref[0, rows, c * LANES:(c + 1) * LANES] = (o * (1.0 - lam_init)).astype(BF16)
        return carry

    lax.fori_loop(0, n_blocks, query_block, 0)


def _diff_attention(q, k, v, bias, lq1, lk1, lq2, lk2, subln_g, lam_init):
    b, s, _ = q.shape
    blk, n_par = DA_BLOCK, DA_PAR
    assert NEAR_CORNER <= blk
    seq_spec = pl.BlockSpec((1, s, n_par * LANES), lambda bi, h: (bi, 0, h))
    vec64 = pl.BlockSpec((1, DA_HEAD_DIM), lambda bi, h: (0, 0))
    state = pltpu.VMEM((n_par, 2 * blk, LANES), F32)
    return pl.pallas_call(
        functools.partial(_da_kernel, blk=blk, lam_init=lam_init, n_par=n_par, n_blocks=s // blk),
        grid=(b, DA_HEADS // n_par),
        in_specs=[seq_spec, seq_spec, seq_spec,
                  pl.BlockSpec((n_par, blk, 2 * blk), lambda bi, h: (h, 0, 0)),
                  vec64, vec64, vec64, vec64,
                  pl.BlockSpec((1, LANES), lambda bi, h: (0, 0))],
        out_specs=seq_spec,
        out_shape=jax.ShapeDtypeStruct(q.shape, BF16),
        scratch_shapes=[state, state, state, pltpu.VMEM((n_par, 2 * blk, blk), F32)],
        compiler_params=_params(2),
        name="diff_attention",
    )(q, k, v, bias, lq1.reshape(1, -1), lk1.reshape(1, -1), lq2.reshape(1, -1), lk2.reshape(1, -1),
      subln_g.reshape(1, -1))


def _sb_kernel(q_ref, k_ref, v_ref, o_ref, later2_sc, r_sc, acc_sc, *, blk, n_par, n_blocks):
    rows = 2 * blk
    jj = lax.broadcasted_iota(I32, (blk, blk), 0)
    ss = lax.broadcasted_iota(I32, (blk, blk), 1)
    later = jnp.where(jj > ss, 1.0, 0.0).astype(BF16)
    later2_sc[...] = jnp.concatenate([later, later], axis=0)
    lane = lax.broadcasted_iota(I32, (blk, LANES), 1)
    lax.fori_loop(0, n_blocks, functools.partial(_sb_query_block, q_ref, k_ref, v_ref, o_ref, later2_sc, r_sc,
                                                 acc_sc, lane, blk, n_par, rows), 0)


def _sb_query_block(q_ref, k_ref, v_ref, o_ref, later2_sc, r_sc, acc_sc, lane, blk, n_par, rows, qi, carry):
    q_rows = pl.ds(pl.multiple_of(qi * blk, blk), blk)
    qs = [_split_streams(q_ref[0, q_rows, c * LANES:(c + 1) * LANES]) for c in range(n_par)]
    later2 = later2_sc[...]
    r_sc[...] = jnp.zeros(r_sc.shape, F32)
    acc_sc[...] = jnp.zeros(acc_sc.shape, F32)

    def blocks(ki, diagonal):
        start = pl.multiple_of(ki * blk, blk)
        pairs = range(n_par)
        zn = [lax.dot_general(qs[c], k_ref[0, pl.ds(start, blk), c * LANES:(c + 1) * LANES],
                              (((1,), (1,)), ((), ())), preferred_element_type=F32) for c in pairs]
        if diagonal:
            r = lax.broadcasted_iota(I32, (rows, blk), 0)
            t = jnp.where(r >= blk, r - blk, r)
            strict = lax.broadcasted_iota(I32, (rows, blk), 1) < t
        log_fail, log_beta, hi_lo = [], [], []
        for c in pairs:
            neg_abs = pltpu.bitcast(pltpu.bitcast(zn[c], jnp.uint32) | jnp.uint32(0x80000000), F32)
            softplus = jnp.log(1.0 + jnp.exp2(neg_abs)) * LOG2E
            lf = jnp.minimum(zn[c], 0.0) - softplus
            log_beta.append(lf - zn[c])
            if diagonal:
                lf = jnp.where(strict, lf, 0.0)
            log_fail.append(lf)
            hi = pltpu.bitcast(pltpu.bitcast(lf, jnp.uint32) & jnp.uint32(0xFFFF0000), F32)
            hi_lo.append(jnp.concatenate([hi.astype(BF16), (lf - hi).astype(BF16)], axis=1))
        suffix = [jnp.dot(hi_lo[c], later2, preferred_element_type=F32) for c in pairs]
        wb = []
        for c in pairs:
            w = jnp.exp2(log_beta[c] + suffix[c])
            if diagonal:
                w = jnp.where(strict, w, 0.0)
            wb.append(w.astype(BF16))
        pv = []
        for c in pairs:
            vb = v_ref[0, pl.ds(start, blk), c * LANES:(c + 1) * LANES]
            pv.append((jnp.dot(wb[c][:blk], vb, preferred_element_type=F32),
                       jnp.dot(wb[c][blk:], vb, preferred_element_type=F32)))
        for c in pairs:
            carry = jnp.exp2(r_sc[c])
            acc_sc[c] += jnp.where(lane < 64, carry[:blk] * pv[c][0], carry[blk:] * pv[c][1])
            r_sc[c] += suffix[c][:, 0:1] + log_fail[c][:, 0:1]

    blocks(qi, True)

    def alive():
        return jnp.max(r_sc[...]) > SB_DEAD_LOG2

    def cond(carry):
        step, live = carry
        return jnp.logical_and(step < qi, live)

    def body(carry):
        step, _ = carry
        blocks(qi - 1 - step, False)
        return step + 1, alive()

    lax.while_loop(cond, body, (jnp.int32(0), alive()))
    for c in range(n_par):
        o_ref[0, q_rows, c * LANES:(c + 1) * LANES] = acc_sc[c].astype(BF16)
    return carry


def _stick_breaking(q, k, v):
    b, s, width = q.shape
    blk, n_par = SB_BLOCK, SB_PAR
    seq_spec = pl.BlockSpec((1, s, n_par * LANES), lambda bi, h: (bi, 0, h))
    return pl.pallas_call(
        functools.partial(_sb_kernel, blk=blk, n_par=n_par, n_blocks=s // blk),
        grid=(b, width // (n_par * LANES)),
        in_specs=[seq_spec, seq_spec, seq_spec],
        out_specs=seq_spec,
        out_shape=jax.ShapeDtypeStruct(q.shape, BF16),
        scratch_shapes=[pltpu.VMEM((2 * blk, blk), BF16), pltpu.VMEM((n_par, 2 * blk, LANES), F32),
                        pltpu.VMEM((n_par, blk, LANES), F32)],
        compiler_params=_params(2),
        name="stick_breaking",
    )(q, k, v)


def _layer_norm(y, g, b):
    mu = jnp.mean(y, axis=1, keepdims=True)
    yc = y - mu
    var = jnp.mean(yc * yc, axis=1, keepdims=True)
    return yc * lax.rsqrt(var + LN_EPS) * g + b


def _first_index_of_max(vals, lane_f):
    m = jnp.max(vals, axis=1, keepdims=True)
    idx = jnp.min(jnp.where(vals == m, lane_f, float(LANES)), axis=1, keepdims=True)
    return m, idx


def _top_experts(logits):
    lane_f = lax.broadcasted_iota(I32, logits.shape, 1).astype(F32)
    is_group = lane_f < MOE_GROUPS
    g_max, g_idx = _first_index_of_max(jnp.where(is_group, logits, NEG_INF), lane_f)
    g_gate = 1.0 / jnp.sum(jnp.where(is_group, jnp.exp(logits - g_max), 0.0), axis=1, keepdims=True)
    first = MOE_GROUPS + MOE_EXPERTS_PER_GROUP * g_idx
    in_group = (lane_f >= first) & (lane_f < first + MOE_EXPERTS_PER_GROUP)
    e_logits = jnp.where(in_group, logits, NEG_INF)
    m1, i1 = _first_index_of_max(e_logits, lane_f)
    m2, i2 = _first_index_of_max(jnp.where(lane_f == i1, NEG_INF, e_logits), lane_f)
    ratio = jnp.exp(m2 - m1)
    w1 = g_gate / (1.0 + ratio)
    return i1, i2, w1, w1 * ratio


def _mix_ln_route_kernel(o_ref, x_ref, wo_ref, g_ref, b_ref, wr_hi_ref, wr_lo_ref, br_ref,
                         x1_ref, x1t_ref, mi_ref, mf_ref, cnt_ref, *, tm):
    step = pl.program_id(0)

    @pl.when(step == 0)
    def _():
        cnt_ref[...] = jnp.zeros(cnt_ref.shape, F32)

    mix = jnp.dot(o_ref[...], wo_ref[...], preferred_element_type=F32)
    x1 = _layer_norm(DEEPNORM_ALPHA * x_ref[...] + mix, g_ref[...], b_ref[...])
    x1_ref[...] = x1
    _store_token_tiles(x1t_ref, x1)

    xh = x1.astype(BF16)
    xl = (x1 - xh.astype(F32)).astype(BF16)
    logits = (jnp.dot(xh, wr_hi_ref[...], preferred_element_type=F32)
              + jnp.dot(xh, wr_lo_ref[...], preferred_element_type=F32)
              + jnp.dot(xl, wr_hi_ref[...], preferred_element_type=F32)) + br_ref[...]
    i1, i2, w1, w2 = _top_experts(logits)

    lane = lax.broadcasted_iota(I32, (tm, LANES), 1)
    lane_f = lane.astype(F32)
    e1 = jnp.where(lane_f == i1, 1.0, 0.0)
    e2 = jnp.where(lane_f == i2, 1.0, 0.0)
    both = e1 + e2
    rr = lax.broadcasted_iota(I32, (tm, tm), 0)
    cc = lax.broadcasted_iota(I32, (tm, tm), 1)
    earlier = jnp.where(rr > cc, 1.0, 0.0).astype(BF16)
    before = cnt_ref[...] + jnp.dot(earlier, both.astype(BF16), preferred_element_type=F32)
    rank1 = jnp.sum(e1 * before, axis=1, keepdims=True)
    rank2 = jnp.sum(e2 * before, axis=1, keepdims=True)
    cnt_ref[...] += jnp.sum(both, axis=0, keepdims=True)

    meta = jnp.where(lane == 0, i1 - MOE_GROUPS,
                     jnp.where(lane == 1, i2 - MOE_GROUPS,
                               jnp.where(lane == 2, rank1, jnp.where(lane == 3, rank2, 0.0))))
    mi_ref[...] = meta.T[:META_ROWS].astype(I32)
    mf_ref[...] = jnp.where(lane == 0, w1, jnp.where(lane == 1, w2, 0.0))


def _mix_ln_route(o2d, x2d, wo, ln_g, ln_b, w_group, b_group, w_expert, b_expert):
    t, d = x2d.shape
    tm = min(ROW_TILE, t)
    pad = LANES - MOE_GROUPS - N_EXPERTS
    wr = jnp.pad(jnp.concatenate([w_group, w_expert], axis=1).astype(F32), ((0, 0), (0, pad)))
    wr_hi = wr.astype(BF16)
    wr_lo = (wr - wr_hi.astype(F32)).astype(BF16)
    br = jnp.pad(jnp.concatenate([b_group, b_expert]).astype(F32), (0, pad)).reshape(1, LANES)
    row = lambda n: pl.BlockSpec((tm, n), lambda i: (i, 0))
    full = lambda r, c: pl.BlockSpec((r, c), lambda i: (0, 0))
    return pl.pallas_call(
        functools.partial(_mix_ln_route_kernel, tm=tm),
        grid=(t // tm,),
        in_specs=[row(o2d.shape[1]), row(d), full(o2d.shape[1], d), full(1, d), full(1, d),
                  full(d, LANES), full(d, LANES), full(1, LANES)],
        out_specs=[row(d), pl.BlockSpec((tm * ROW_GROUP, LANES), lambda i: (i, 0)),
                   pl.BlockSpec((META_ROWS, tm), lambda i: (0, i)), row(LANES), full(1, LANES)],
        out_shape=[jax.ShapeDtypeStruct((t, d), F32), jax.ShapeDtypeStruct((t * ROW_GROUP, LANES), F32),
                   jax.ShapeDtypeStruct((META_ROWS, t), I32),
                   jax.ShapeDtypeStruct((t, LANES), F32), jax.ShapeDtypeStruct((1, LANES), F32)],
        compiler_params=_params(1),
        name="mix_ln_route",
    )(o2d, x2d, wo.astype(BF16), ln_g.reshape(1, d).astype(F32), ln_b.reshape(1, d).astype(F32),
      wr_hi, wr_lo, br)


def _load_token_tiles(ref, rows):
    return jnp.concatenate([ref[pl.ds(s, rows, stride=ROW_GROUP), :] for s in range(ROW_GROUP)], axis=1)


def _store_token_tiles(ref, val):
    for s in range(ROW_GROUP):
        ref[pl.ds(s, val.shape[0], stride=ROW_GROUP), :] = val[:, s * LANES:(s + 1) * LANES]


def _start_row_copies(tm, copy_of):
    def group(g, carry):
        for r in range(ROW_GROUP):
            for slot in range(2):
                copy_of(g * ROW_GROUP + r, slot).start(priority=slot)
        return carry

    lax.fori_loop(0, tm // ROW_GROUP, group, 0)


def _wait_row_copies(tm, copy_of):
    def group(g, carry):
        for _ in range(2 * ROW_GROUP):
            copy_of(0, 0).wait()
        return carry

    lax.fori_loop(0, tm // ROW_GROUP, group, 0)


def _row(ref, row):
    return ref.at[pl.ds(pl.multiple_of(row * ROW_GROUP, ROW_GROUP), ROW_GROUP)]


def _dispatch_kernel(pad_base_ref, pad_len_ref, tail_ref, pos_ref, x_ref, xs_ref, zeros_sc, sem, *, tm, tile):
    @pl.when(pl.program_id(0) == 0)
    def _():
        zeros_sc[...] = jnp.zeros(zeros_sc.shape, F32)

        def pad_rows(e, carry):
            def one(r, c, wait):
                cp = pltpu.make_async_copy(_row(zeros_sc, 0), _row(xs_ref, pad_base_ref[e] + r), sem)
                cp.wait() if wait else cp.start()
                return c
            lax.fori_loop(0, pad_len_ref[e], functools.partial(one, wait=False), 0)
            lax.fori_loop(0, pad_len_ref[e], functools.partial(one, wait=True), 0)
            return carry

        lax.fori_loop(0, N_EXPERTS, pad_rows, 0)

        def tail_tile(i, carry):
            n = tile * ROW_GROUP
            cp = pltpu.make_async_copy(zeros_sc, xs_ref.at[pl.ds(pl.multiple_of(i * n, n), n)], sem)
            cp.start()
            cp.wait()
            return carry

        lax.fori_loop(tail_ref[0], tail_ref[1], tail_tile, 0)

    copy_of = lambda row, slot: pltpu.make_async_copy(_row(x_ref, row), _row(xs_ref, pos_ref[slot * tm + row]), sem)
    _start_row_copies(tm, copy_of)
    _wait_row_copies(tm, copy_of)


def _pos_spec(tm):
    return pl.BlockSpec((2 * tm,), lambda i, *prefetch: (i,), memory_space=pltpu.SMEM)


def _token_tile_spec(rows, index_map):
    return pl.BlockSpec((rows * ROW_GROUP, LANES), index_map)


def _dispatch(plan, x1t, n_rows):
    t = x1t.shape[0] // ROW_GROUP
    tm = min(DISPATCH_TILE, t)
    grid_spec = pltpu.PrefetchScalarGridSpec(
        num_scalar_prefetch=3,
        grid=(t // tm,),
        in_specs=[_pos_spec(tm), _token_tile_spec(tm, lambda i, *prefetch: (i, 0))],
        out_specs=pl.BlockSpec(memory_space=pl.ANY),
        scratch_shapes=[pltpu.VMEM((FFN_TILE * ROW_GROUP, LANES), F32), pltpu.SemaphoreType.DMA(())],
    )
    return pl.pallas_call(
        functools.partial(_dispatch_kernel, tm=tm, tile=FFN_TILE),
        grid_spec=grid_spec,
        out_shape=jax.ShapeDtypeStruct((n_rows * ROW_GROUP, LANES), F32),
        compiler_params=_params(1),
        name="moe_dispatch",
    )(plan["pad_base"], plan["pad_len"], plan["tail"], plan["pos_dispatch"], x1t)


def _ffn_kernel(tile_expert_ref, n_used_ref, xs_ref, w1_ref, w3_ref, w2_ref, ys_ref, w1_sc, w3_sc, w2_sc):
    i = pl.program_id(0)
    used = i < n_used_ref[0]
    new_expert = jnp.logical_or(i == 0, tile_expert_ref[i] != tile_expert_ref[jnp.maximum(i - 1, 0)])

    @pl.when(jnp.logical_and(used, new_expert))
    def _():
        w1_sc[...] = w1_ref[0, 0].astype(BF16)
        w3_sc[...] = w3_ref[0, 0].astype(BF16)
        w2_sc[...] = w2_ref[0, 0].astype(BF16)

    @pl.when(used)
    def _():
        x = _load_token_tiles(xs_ref, xs_ref.shape[0] // ROW_GROUP).astype(BF16)
        a = jnp.dot(x, w1_sc[...], preferred_element_type=F32)
        b = jnp.dot(x, w3_sc[...], preferred_element_type=F32)
        h = (a * jax.nn.sigmoid(a) * b).astype(BF16)
        _store_token_tiles(ys_ref, jnp.dot(h, w2_sc[...], preferred_element_type=F32))

    @pl.when(jnp.logical_not(used))
    def _():
        ys_ref[...] = jnp.zeros(ys_ref.shape, F32)


def _expert_ffn(plan, xs, w1, w3, w2, layer):
    n_rows = xs.shape[0] // ROW_GROUP
    d, hidden = w1.shape[2], w1.shape[3]
    assert d == ROW_GROUP * LANES
    tm = FFN_TILE
    x_spec = _token_tile_spec(tm, lambda i, te, nu: (jnp.minimum(i, nu[0] - 1), 0))
    w_in = pl.BlockSpec((1, 1, d, hidden), lambda i, te, nu: (layer, te[i], 0, 0))
    w_out = pl.BlockSpec((1, 1, hidden, d), lambda i, te, nu: (layer, te[i], 0, 0))
    grid_spec = pltpu.PrefetchScalarGridSpec(
        num_scalar_prefetch=2,
        grid=(n_rows // tm,),
        in_specs=[x_spec, w_in, w_in, w_out],
        out_specs=_token_tile_spec(tm, lambda i, te, nu: (i, 0)),
        scratch_shapes=[pltpu.VMEM((d, hidden), BF16), pltpu.VMEM((d, hidden), BF16),
                        pltpu.VMEM((hidden, d), BF16)],
    )
    return pl.pallas_call(
        _ffn_kernel,
        grid_spec=grid_spec,
        out_shape=jax.ShapeDtypeStruct(xs.shape, F32),
        compiler_params=_params(1),
        name="moe_experts",
    )(plan["tile_expert"], plan["n_used"], xs, w1.astype(F32), w3.astype(F32), w2.astype(F32))


def _combine_kernel(pos_ref, pos_next_ref, x1_ref, mf_ref, ys_ref, g_ref, b_ref, out_ref, ybuf, sem, *, tm):
    i = pl.program_id(0)
    cur = lax.rem(i, 2)

    def gather(positions, buf):
        return lambda row, slot: pltpu.make_async_copy(_row(ys_ref, positions[slot * tm + row]),
                                                       _row(ybuf.at[buf, slot], row), sem.at[buf])

    @pl.when(i == 0)
    def _():
        _start_row_copies(tm, gather(pos_ref, cur))

    @pl.when(i + 1 < pl.num_programs(0))
    def _():
        _start_row_copies(tm, gather(pos_next_ref, 1 - cur))

    _wait_row_copies(tm, gather(pos_ref, cur))
    gates = mf_ref[...]
    y = (gates[:, 0:1] * _load_token_tiles(ybuf.at[cur, 0], tm)
         + gates[:, 1:2] * _load_token_tiles(ybuf.at[cur, 1], tm))
    out_ref[...] = _layer_norm(DEEPNORM_ALPHA * x1_ref[...] + y, g_ref[...], b_ref[...])


def _combine(plan, x1, meta_f, ys, ln_g, ln_b):
    t, d = x1.shape
    tm = min(ROW_TILE, t)
    row = lambda n: pl.BlockSpec((tm, n), lambda i: (i, 0))
    full = pl.BlockSpec((1, d), lambda i: (0, 0))
    last = t // tm - 1
    pos_next = pl.BlockSpec((2 * tm,), lambda i: (jnp.minimum(i + 1, last),), memory_space=pltpu.SMEM)
    return pl.pallas_call(
        functools.partial(_combine_kernel, tm=tm),
        grid=(t // tm,),
        in_specs=[_pos_spec(tm), pos_next, row(d), row(LANES), pl.BlockSpec(memory_space=pl.ANY), full, full],
        out_specs=row(d),
        out_shape=jax.ShapeDtypeStruct((t, d), F32),
        scratch_shapes=[pltpu.VMEM((2, 2, tm * ROW_GROUP, LANES), F32), pltpu.SemaphoreType.DMA((2,))],
        compiler_params=_params(1),
        name="moe_combine",
    )(plan["pos"], plan["pos"], x1, meta_f, ys, ln_g.reshape(1, d).astype(F32), ln_b.reshape(1, d).astype(F32))


def _routing_plan(meta_i, counts_row, n_tiles):
    counts = counts_row[0, MOE_GROUPS:MOE_GROUPS + N_EXPERTS].astype(I32)
    padded = ((counts + FFN_TILE - 1) // FFN_TILE) * FFN_TILE
    ends = jnp.cumsum(padded)
    offsets = ends - padded
    experts = jnp.arange(N_EXPERTS, dtype=I32)
    expert, rank = meta_i[0:2], meta_i[2:4]
    pos = rank + jnp.sum(jnp.where(expert[..., None] == experts, offsets, 0), axis=-1)
    tile_ends = ends // FFN_TILE
    tile_expert = jnp.sum(jnp.arange(n_tiles, dtype=I32)[:, None] >= tile_ends[None, :], axis=1)
    n_used = tile_ends[-1:].astype(I32)
    pos = pos.astype(I32)
    t = pos.shape[1]
    by_tile = lambda tm: pos.reshape(2, -1, tm).transpose(1, 0, 2).reshape(-1)
    return dict(pos=by_tile(min(ROW_TILE, t)), pos_dispatch=by_tile(min(DISPATCH_TILE, t)), tile_expert=jnp.minimum(tile_expert, N_EXPERTS - 1).astype(I32),
                n_used=n_used, pad_base=(offsets + counts).astype(I32), pad_len=(padded - counts).astype(I32),
                tail=jnp.concatenate([n_used, jnp.full((1,), n_tiles, I32)]))


def _hier_moe_ln(x1, x1t, meta_i, meta_f, counts_row, w1, w3, w2, layer, ln_g, ln_b):
    t, d = x1.shape
    n_tiles = (2 * t) // FFN_TILE + N_EXPERTS
    plan = _routing_plan(meta_i, counts_row, n_tiles)
    xs = _dispatch(plan, x1t, n_tiles * FFN_TILE)
    ys = _expert_ffn(plan, xs, w1, w3, w2, layer)
    return _combine(plan, x1, meta_f, ys, ln_g, ln_b)


def kernel(x, rel_table, da_wq, da_wk, da_wv, da_wo, da_lq1, da_lk1, da_lq2, da_lk2, da_subln_g, sb_wq, sb_wk,
           sb_wv, sb_wo, ln_mix_g, ln_mix_b, ln_ffn_g, ln_ffn_b, moe_w_group, moe_b_group, moe_w_expert,
           moe_b_expert, moe_w1, moe_w3, moe_w2):
    b, s, d = x.shape
    t = b * s
    assert s % DA_BLOCK == 0 and s % SB_BLOCK == 0 and (2 * t) % FFN_TILE == 0
    assert t % min(ROW_TILE, t) == 0 and t % min(DISPATCH_TILE, t) == 0
    xf = x.reshape(t, d).astype(F32)
    bias = _bias_tiles(rel_table, DA_BLOCK)
    for layer in range(DEPTH):
        j = layer // 2
        if layer % 2 == 0:
            q, k, v = _qkv_proj(xf, da_wq[j], da_wk[j], da_wv[j], DA_HEAD_DIM ** -0.5 * LOG2E)
            shape = (b, s, q.shape[1])
            lam_init = 0.8 - 0.6 * math.exp(-0.3 * layer)
            o = _diff_attention(q.reshape(shape), k.reshape(shape), v.reshape(shape), bias,
                                da_lq1[j], da_lk1[j], da_lq2[j], da_lk2[j], da_subln_g[j], lam_init)
            wo = da_wo[j]
        else:
            q, k, v = _qkv_proj(xf, sb_wq[j], sb_wk[j], sb_wv[j], -(SB_HEAD_DIM ** -0.5) * LOG2E)
            shape = (b, s, q.shape[1])
            o = _stick_breaking(q.reshape(shape), k.reshape(shape), v.reshape(shape))
            wo = sb_wo[j]
        x1, x1t, meta_i, meta_f, counts = _mix_ln_route(
            o.reshape(t, -1), xf, wo, ln_mix_g[layer], ln_mix_b[layer],
            moe_w_group[layer], moe_b_group[layer], moe_w_expert[layer], moe_b_expert[layer])
        xf = _hier_moe_ln(x1, x1t, meta_i, meta_f, counts, moe_w1, moe_w3, moe_w2, layer,
                          ln_ffn_g[layer], ln_ffn_b[layer])
    return xf.reshape(b, s, d).astype(x.dtype)
```

```python
import functools
import math

import jax
import jax.numpy as jnp
from jax import lax
from jax.experimental import pallas as pl
from jax.experimental.pallas import tpu as pltpu

F32 = jnp.float32
BF16 = jnp.bfloat16
I32 = jnp.int32

DEPTH = 2
DA_HEADS = 8
DA_HEAD_DIM = 64
SB_HEAD_DIM = 64
REL_BUCKETS = 32
REL_MAX_DIST = 128
MOE_GROUPS = 4
MOE_EXPERTS_PER_GROUP = 8
N_EXPERTS = MOE_GROUPS * MOE_EXPERTS_PER_GROUP
DEEPNORM_ALPHA = (2 * DEPTH) ** 0.25
LN_EPS = 1e-5
RMS_EPS = 1e-6
NEG_INF = -1e30
LOG2E = 1.4426950408889634
SB_DEAD_LOG2 = -160.0

LANES = 128
DA_BLOCK = 512
DA_PAR = 2
SB_BLOCK = 256
SB_PAR = 4
ROW_TILE = 1024
COMBINE_TILE = 512
DISPATCH_TILE = 1024
FFN_TILE = 256
META_ROWS = 8
ROW_GROUP = 8
VMEM_LIMIT = 48 * 1024 * 1024


def _params(n_axes, vmem=VMEM_LIMIT):
    return pltpu.CompilerParams(dimension_semantics=("arbitrary",) * n_axes, vmem_limit_bytes=vmem)


def _proj_kernel(x_ref, wq_ref, wk_ref, wv_ref, q_ref, k_ref, v_ref, *, q_scale):
    x = x_ref[...].astype(BF16)
    q = jnp.dot(x, wq_ref[...], preferred_element_type=F32)
    q_ref[...] = (q * q_scale).astype(BF16)
    k_ref[...] = jnp.dot(x, wk_ref[...], preferred_element_type=F32).astype(BF16)
    v_ref[...] = jnp.dot(x, wv_ref[...], preferred_element_type=F32).astype(BF16)


def _qkv_proj(x2d, wq, wk, wv, q_scale):
    t, d = x2d.shape
    n = wq.shape[1]
    tm = min(ROW_TILE, t)
    row_in = pl.BlockSpec((tm, d), lambda i: (i, 0))
    w_spec = pl.BlockSpec((d, n), lambda i: (0, 0))
    row_out = pl.BlockSpec((tm, n), lambda i: (i, 0))
    return pl.pallas_call(
        functools.partial(_proj_kernel, q_scale=q_scale),
        grid=(t // tm,),
        in_specs=[row_in, w_spec, w_spec, w_spec],
        out_specs=[row_out, row_out, row_out],
        out_shape=[jax.ShapeDtypeStruct((t, n), BF16)] * 3,
        compiler_params=_params(1),
        name="qkv_proj",
    )(x2d, wq.astype(BF16), wk.astype(BF16), wv.astype(BF16))


def _bias_kernel(tab_ref, out_ref, *, blk):
    h = pl.program_id(0)
    max_exact = REL_BUCKETS // 2
    far = tab_ref[REL_BUCKETS - 1, h]
    far_distance = _first_far_distance()
    for a in range(blk // LANES):
        for b in range(2 * blk // LANES):
            rows, cols = slice(a * LANES, (a + 1) * LANES), slice(b * LANES, (b + 1) * LANES)
            n_low = blk + a * LANES - (b * LANES + LANES - 1)
            n_high = blk + a * LANES + LANES - 1 - b * LANES
            if n_low >= far_distance:
                out_ref[0, rows, cols] = jnp.zeros((LANES, LANES), F32)
                continue
            if n_high < 0:
                out_ref[0, rows, cols] = jnp.full((LANES, LANES), NEG_INF, F32)
                continue
            i = lax.broadcasted_iota(I32, (LANES, LANES), 0)
            j = lax.broadcasted_iota(I32, (LANES, LANES), 1)
            n = n_low + (LANES - 1) + i - j
            nn = jnp.maximum(n, 0)
            nf = jnp.maximum(nn, 1).astype(F32)
            large = max_exact + (jnp.log(nf * (1.0 / max_exact))
                                 * ((REL_BUCKETS - max_exact) / math.log(REL_MAX_DIST / max_exact))).astype(I32)
            large = jnp.minimum(large, REL_BUCKETS - 1)
            bucket = jnp.where(nn < max_exact, nn, large)
            acc = jnp.zeros((LANES, LANES), F32)
            for t in range(REL_BUCKETS):
                acc = jnp.where(bucket == t, tab_ref[t, h] - far, acc)
            out_ref[0, rows, cols] = jnp.where(n >= 0, acc * LOG2E, NEG_INF)


def _bias_tiles(rel_table, blk):
    heads = rel_table.shape[1]
    return pl.pallas_call(
        functools.partial(_bias_kernel, blk=blk),
        grid=(heads,),
        in_specs=[pl.BlockSpec(memory_space=pltpu.SMEM)],
        out_specs=pl.BlockSpec((1, blk, 2 * blk), lambda h: (h, 0, 0)),
        out_shape=jax.ShapeDtypeStruct((heads, blk, 2 * blk), F32),
        compiler_params=_params(1),
        name="rel_bias_tiles",
    )(rel_table.astype(F32))


def _split_streams(q):
    lane = lax.broadcasted_iota(I32, q.shape, 1)
    zero = jnp.zeros_like(q)
    return jnp.concatenate([jnp.where(lane < 64, q, zero), jnp.where(lane >= 64, q, zero)], axis=0)


def _first_far_distance():
    max_exact = REL_BUCKETS // 2
    for n in range(max_exact, 16 * REL_MAX_DIST):
        if max_exact + int(math.log(n / max_exact) / math.log(REL_MAX_DIST / max_exact) * (REL_BUCKETS - max_exact)) \
                >= REL_BUCKETS - 1:
            return n
    raise ValueError("no distance reaches the last bucket")


NEAR_CORNER = -(-_first_far_distance() // LANES) * LANES


def _da_kernel(q_ref, k_ref, v_ref, bias_ref, lq1_ref, lk1_ref, lq2_ref, lk2_ref, g_ref, o_ref,
               m_sc, l_sc, acc_sc, s_sc, *, blk, lam_init, n_par, n_blocks):
    heads = range(n_par)
    lam = (jnp.exp(jnp.sum(lq1_ref[...] * lk1_ref[...], axis=1, keepdims=True))
           - jnp.exp(jnp.sum(lq2_ref[...] * lk2_ref[...], axis=1, keepdims=True)) + lam_init)

    def split_q(qi):
        start = pl.multiple_of(qi * blk, blk)
        return [_split_streams(q_ref[0, pl.ds(start, blk), c * LANES:(c + 1) * LANES]) for c in heads]

    def logits(qs, ki):
        start = pl.multiple_of(ki * blk, blk)
        return [lax.dot_general(qs[c], k_ref[0, pl.ds(start, blk), c * LANES:(c + 1) * LANES],
                                (((1,), (1,)), ((), ())), preferred_element_type=F32) for c in heads]

    def store_logits(vals):
        for c in heads:
            s_sc[c] = vals[c]

    def consume(ki, bias):
        start = pl.multiple_of(ki * blk, blk)
        alpha, p = [], []
        for c in heads:
            if bias == "near":
                corner = bias_ref[c, 0:NEAR_CORNER, blk - NEAR_CORNER:blk]
                for comp in range(2):
                    r0 = comp * blk
                    s_sc[c, r0:r0 + NEAR_CORNER, blk - NEAR_CORNER:blk] += corner
            sc = s_sc[c]
            if bias == "diagonal":
                tile = bias_ref[c, :, blk:2 * blk]
                sc = sc + jnp.concatenate([tile, tile], axis=0)
            m_prev = m_sc[c]
            m_next = jnp.maximum(m_prev, jnp.max(sc, axis=1, keepdims=True))
            m_sc[c] = m_next
            alpha.append(jnp.exp2(m_prev - m_next))
            pc = jnp.exp2(sc - jnp.concatenate([m_next] * (blk // LANES), axis=1))
            l_sc[c] = alpha[c] * l_sc[c] + sum(pc[:, n * LANES:(n + 1) * LANES] for n in range(blk // LANES))
            p.append(pc.astype(BF16))
        pv = [jnp.dot(p[c], v_ref[0, pl.ds(start, blk), c * LANES:(c + 1) * LANES],
                      preferred_element_type=F32) for c in heads]
        for c in heads:
            acc_sc[c] = alpha[c] * acc_sc[c] + pv[c]

    store_logits(logits(split_q(0), 0))

    def query_block(qi, carry):
        qs = split_q(qi)
        m_sc[...] = jnp.full(m_sc.shape, NEG_INF, F32)
        l_sc[...] = jnp.zeros(l_sc.shape, F32)
        acc_sc[...] = jnp.zeros(acc_sc.shape, F32)

        def step(ki, bias):
            nxt = logits(qs, ki + 1)
            consume(ki, bias)
            store_logits(nxt)

        def far_body(ki, c):
            step(ki, None)
            return c

        lax.fori_loop(0, jnp.maximum(qi - 1, 0), far_body, 0)

        @pl.when(qi >= 1)
        def _():
            step(qi - 1, "near")

        nxt = logits(split_q(jnp.minimum(qi + 1, n_blocks - 1)), 0)
        consume(qi, "diagonal")
        store_logits(nxt)

        rows = pl.ds(pl.multiple_of(qi * blk, blk), blk)
        for c in heads:
            inv_l = 1.0 / jnp.sum(l_sc[c], axis=1, keepdims=True)
            acc = acc_sc[c]
            o = acc[:blk] * inv_l[:blk] - lam * (acc[blk:] * inv_l[blk:])
            o = o * lax.rsqrt(jnp.mean(o * o, axis=1, keepdims=True) + RMS_EPS) * g_ref[...]
            o_ref[0, rows, c * LANES:(c + 1) * LANES] = (o * (1.0 - lam_init)).astype(BF16)
        return carry

    lax.fori_loop(0, n_blocks, query_block, 0)


def _diff_attention(q, k, v, bias, lq1, lk1, lq2, lk2, subln_g, lam_init):
    b, s, _ = q.shape
    blk, n_par = DA_BLOCK, DA_PAR
    assert NEAR_CORNER <= blk
    seq_spec = pl.BlockSpec((1, s, n_par * LANES), lambda bi, h: (bi, 0, h))
    vec64 = pl.BlockSpec((1, DA_HEAD_DIM), lambda bi, h: (0, 0))
    state = pltpu.VMEM((n_par, 2 * blk, LANES), F32)
    return pl.pallas_call(
        functools.partial(_da_kernel, blk=blk, lam_init=lam_init, n_par=n_par, n_blocks=s // blk),
        grid=(b, DA_HEADS // n_par),
        in_specs=[seq_spec, seq_spec, seq_spec,
                  pl.BlockSpec((n_par, blk, 2 * blk), lambda bi, h: (h, 0, 0)),
                  vec64, vec64, vec64, vec64,
                  pl.BlockSpec((1, LANES), lambda bi, h: (0, 0))],
        out_specs=seq_spec,
        out_shape=jax.ShapeDtypeStruct(q.shape, BF16),
        scratch_shapes=[state, state, state, pltpu.VMEM((n_par, 2 * blk, blk), F32)],
        compiler_params=_params(2),
        name="diff_attention",
    )(q, k, v, bias, lq1.reshape(1, -1), lk1.reshape(1, -1), lq2.reshape(1, -1), lk2.reshape(1, -1),
      subln_g.reshape(1, -1))


def _sb_kernel(q_ref, k_ref, v_ref, o_ref, later2_sc, r_sc, acc_sc, *, blk, n_par, n_blocks):
    rows = 2 * blk
    jj = lax.broadcasted_iota(I32, (blk, blk), 0)
    ss = lax.broadcasted_iota(I32, (blk, blk), 1)
    later = jnp.where(jj > ss, 1.0, 0.0).astype(BF16)
    later2_sc[...] = jnp.concatenate([later, later], axis=0)
    lane = lax.broadcasted_iota(I32, (blk, LANES), 1)
    lax.fori_loop(0, n_blocks, functools.partial(_sb_query_block, q_ref, k_ref, v_ref, o_ref, later2_sc, r_sc,
                                                 acc_sc, lane, blk, n_par, rows), 0)


def _sb_query_block(q_ref, k_ref, v_ref, o_ref, later2_sc, r_sc, acc_sc, lane, blk, n_par, rows, qi, carry):
    q_rows = pl.ds(pl.multiple_of(qi * blk, blk), blk)
    qs = [_split_streams(q_ref[0, q_rows, c * LANES:(c + 1) * LANES]) for c in range(n_par)]
    later2 = later2_sc[...]
    r_sc[...] = jnp.zeros(r_sc.shape, F32)
    acc_sc[...] = jnp.zeros(acc_sc.shape, F32)

    def blocks(ki, diagonal):
        start = pl.multiple_of(ki * blk, blk)
        pairs = range(n_par)
        zn = [lax.dot_general(qs[c], k_ref[0, pl.ds(start, blk), c * LANES:(c + 1) * LANES],
                              (((1,), (1,)), ((), ())), preferred_element_type=F32) for c in pairs]
        if diagonal:
            r = lax.broadcasted_iota(I32, (rows, blk), 0)
            t = jnp.where(r >= blk, r - blk, r)
            strict = lax.broadcasted_iota(I32, (rows, blk), 1) < t
        log_fail, log_beta, hi_lo = [], [], []
        for c in pairs:
            neg_abs = pltpu.bitcast(pltpu.bitcast(zn[c], jnp.uint32) | jnp.uint32(0x80000000), F32)
            softplus = jnp.log(1.0 + jnp.exp2(neg_abs)) * LOG2E
            lf = jnp.minimum(zn[c], 0.0) - softplus
            log_beta.append(lf - zn[c])
            if diagonal:
                lf = jnp.where(strict, lf, 0.0)
            log_fail.append(lf)
            hi = pltpu.bitcast(pltpu.bitcast(lf, jnp.uint32) & jnp.uint32(0xFFFF0000), F32)
            hi_lo.append(jnp.concatenate([hi.astype(BF16), (lf - hi).astype(BF16)], axis=1))
        suffix = [jnp.dot(hi_lo[c], later2, preferred_element_type=F32) for c in pairs]
        wb = []
        for c in pairs:
            w = jnp.exp2(log_beta[c] + suffix[c])
            if diagonal:
                w = jnp.where(strict, w, 0.0)
            wb.append(w.astype(BF16))
        pv = []
        for c in pairs:
            vb = v_ref[0, pl.ds(start, blk), c * LANES:(c + 1) * LANES]
            pv.append((jnp.dot(wb[c][:blk], vb, preferred_element_type=F32),
                       jnp.dot(wb[c][blk:], vb, preferred_element_type=F32)))
        for c in pairs:
            carry = jnp.exp2(r_sc[c])
            acc_sc[c] += jnp.where(lane < 64, carry[:blk] * pv[c][0], carry[blk:] * pv[c][1])
            r_sc[c] += suffix[c][:, 0:1] + log_fail[c][:, 0:1]

    blocks(qi, True)

    def alive():
        return jnp.max(r_sc[...]) > SB_DEAD_LOG2

    def cond(carry):
        step, live = carry
        return jnp.logical_and(step < qi, live)

    def body(carry):
        step, _ = carry
        blocks(qi - 1 - step, False)
        return step + 1, alive()

    lax.while_loop(cond, body, (jnp.int32(0), alive()))
    for c in range(n_par):
        o_ref[0, q_rows, c * LANES:(c + 1) * LANES] = acc_sc[c].astype(BF16)
    return carry


def _stick_breaking(q, k, v):
    b, s, width = q.shape
    blk, n_par = SB_BLOCK, SB_PAR
    seq_spec = pl.BlockSpec((1, s, n_par * LANES), lambda bi, h: (bi, 0, h))
    return pl.pallas_call(
        functools.partial(_sb_kernel, blk=blk, n_par=n_par, n_blocks=s // blk),
        grid=(b, width // (n_par * LANES)),
        in_specs=[seq_spec, seq_spec, seq_spec],
        out_specs=seq_spec,
        out_shape=jax.ShapeDtypeStruct(q.shape, BF16),
        scratch_shapes=[pltpu.VMEM((2 * blk, blk), BF16), pltpu.VMEM((n_par, 2 * blk, LANES), F32),
                        pltpu.VMEM((n_par, blk, LANES), F32)],
        compiler_params=_params(2),
        name="stick_breaking",
    )(q, k, v)


def _layer_norm(y, g, b):
    mu = jnp.mean(y, axis=1, keepdims=True)
    yc = y - mu
    var = jnp.mean(yc * yc, axis=1, keepdims=True)
    return yc * lax.rsqrt(var + LN_EPS) * g + b


def _first_index_of_max(vals, lane_f):
    m = jnp.max(vals, axis=1, keepdims=True)
    idx = jnp.min(jnp.where(vals == m, lane_f, float(LANES)), axis=1, keepdims=True)
    return m, idx


def _top_experts(logits):
    lane_f = lax.broadcasted_iota(I32, logits.shape, 1).astype(F32)
    is_group = lane_f < MOE_GROUPS
    g_max, g_idx = _first_index_of_max(jnp.where(is_group, logits, NEG_INF), lane_f)
    g_gate = 1.0 / jnp.sum(jnp.where(is_group, jnp.exp(logits - g_max), 0.0), axis=1, keepdims=True)
    first = MOE_GROUPS + MOE_EXPERTS_PER_GROUP * g_idx
    in_group = (lane_f >= first) & (lane_f < first + MOE_EXPERTS_PER_GROUP)
    e_logits = jnp.where(in_group, logits, NEG_INF)
    m1, i1 = _first_index_of_max(e_logits, lane_f)
    m2, i2 = _first_index_of_max(jnp.where(lane_f == i1, NEG_INF, e_logits), lane_f)
    ratio = jnp.exp(m2 - m1)
    w1 = g_gate / (1.0 + ratio)
    return i1, i2, w1, w1 * ratio


def _mix_ln_route_kernel(o_ref, x_ref, wo_ref, g_ref, b_ref, wr_hi_ref, wr_lo_ref, br_ref,
                         x1_ref, x1t_ref, mi_ref, mf_ref, cnt_ref, *, tm):
    step = pl.program_id(0)

    @pl.when(step == 0)
    def _():
        cnt_ref[...] = jnp.zeros(cnt_ref.shape, F32)

    mix = jnp.dot(o_ref[...], wo_ref[...], preferred_element_type=F32)
    x1 = _layer_norm(DEEPNORM_ALPHA * x_ref[...] + mix, g_ref[...], b_ref[...])
    x1_ref[...] = x1
    _store_token_tiles(x1t_ref, x1)

    xh = x1.astype(BF16)
    xl = (x1 - xh.astype(F32)).astype(BF16)
    logits = (jnp.dot(xh, wr_hi_ref[...], preferred_element_type=F32)
              + jnp.dot(xh, wr_lo_ref[...], preferred_element_type=F32)
              + jnp.dot(xl, wr_hi_ref[...], preferred_element_type=F32)) + br_ref[...]
    i1, i2, w1, w2 = _top_experts(logits)

    lane = lax.broadcasted_iota(I32, (tm, LANES), 1)
    lane_f = lane.astype(F32)
    e1 = jnp.where(lane_f == i1, 1.0, 0.0)
    e2 = jnp.where(lane_f == i2, 1.0, 0.0)
    both = e1 + e2
    rr = lax.broadcasted_iota(I32, (tm, tm), 0)
    cc = lax.broadcasted_iota(I32, (tm, tm), 1)
    earlier = jnp.where(rr > cc, 1.0, 0.0).astype(BF16)
    before = cnt_ref[...] + jnp.dot(earlier, both.astype(BF16), preferred_element_type=F32)
    rank1 = jnp.sum(e1 * before, axis=1, keepdims=True)
    rank2 = jnp.sum(e2 * before, axis=1, keepdims=True)
    cnt_ref[...] += jnp.sum(both, axis=0, keepdims=True)

    meta = jnp.where(lane == 0, i1 - MOE_GROUPS,
                     jnp.where(lane == 1, i2 - MOE_GROUPS,
                               jnp.where(lane == 2, rank1, jnp.where(lane == 3, rank2, 0.0))))
    mi_ref[...] = meta.T[:META_ROWS].astype(I32)
    mf_ref[...] = jnp.where(lane == 0, w1, jnp.where(lane == 1, w2, 0.0))


def _mix_ln_route(o2d, x2d, wo, ln_g, ln_b, w_group, b_group, w_expert, b_expert):
    t, d = x2d.shape
    tm = min(ROW_TILE, t)
    pad = LANES - MOE_GROUPS - N_EXPERTS
    wr = jnp.pad(jnp.concatenate([w_group, w_expert], axis=1).astype(F32), ((0, 0), (0, pad)))
    wr_hi = wr.astype(BF16)
    wr_lo = (wr - wr_hi.astype(F32)).astype(BF16)
    br = jnp.pad(jnp.concatenate([b_group, b_expert]).astype(F32), (0, pad)).reshape(1, LANES)
    row = lambda n: pl.BlockSpec((tm, n), lambda i: (i, 0))
    full = lambda r, c: pl.BlockSpec((r, c), lambda i: (0, 0))
    return pl.pallas_call(
        functools.partial(_mix_ln_route_kernel, tm=tm),
        grid=(t // tm,),
        in_specs=[row(o2d.shape[1]), row(d), full(o2d.shape[1], d), full(1, d), full(1, d),
                  full(d, LANES), full(d, LANES), full(1, LANES)],
        out_specs=[row(d), pl.BlockSpec((tm * ROW_GROUP, LANES), lambda i: (i, 0)),
                   pl.BlockSpec((META_ROWS, tm), lambda i: (0, i)), row(LANES), full(1, LANES)],
        out_shape=[jax.ShapeDtypeStruct((t, d), F32), jax.ShapeDtypeStruct((t * ROW_GROUP, LANES), F32),
                   jax.ShapeDtypeStruct((META_ROWS, t), I32),
                   jax.ShapeDtypeStruct((t, LANES), F32), jax.ShapeDtypeStruct((1, LANES), F32)],
        compiler_params=_params(1),
        name="mix_ln_route",
    )(o2d, x2d, wo.astype(BF16), ln_g.reshape(1, d).astype(F32), ln_b.reshape(1, d).astype(F32),
      wr_hi, wr_lo, br)


def _load_token_tiles(ref, rows):
    return jnp.concatenate([ref[pl.ds(s, rows, stride=ROW_GROUP), :] for s in range(ROW_GROUP)], axis=1)


def _store_token_tiles(ref, val):
    for s in range(ROW_GROUP):
        ref[pl.ds(s, val.shape[0], stride=ROW_GROUP), :] = val[:, s * LANES:(s + 1) * LANES]


def _start_row_copies(tm, copy_of):
    def group(g, carry):
        for r in range(ROW_GROUP):
            for slot in range(2):
                copy_of(g * ROW_GROUP + r, slot).start(priority=slot)
        return carry

    lax.fori_loop(0, tm // ROW_GROUP, group, 0)


def _wait_row_copies(tm, copy_of):
    def group(g, carry):
        for _ in range(2 * ROW_GROUP):
            copy_of(0, 0).wait()
        return carry

    lax.fori_loop(0, tm // ROW_GROUP, group, 0)


def _row(ref, row):
    return ref.at[pl.ds(pl.multiple_of(row * ROW_GROUP, ROW_GROUP), ROW_GROUP)]


def _dispatch_kernel(pad_base_ref, pad_len_ref, tail_ref, pos_ref, x_ref, xs_ref, zeros_sc, sem, *, tm, tile):
    @pl.when(pl.program_id(0) == 0)
    def _():
        zeros_sc[...] = jnp.zeros(zeros_sc.shape, F32)

        def pad_rows(e, carry):
            def one(r, c, wait):
                cp = pltpu.make_async_copy(_row(zeros_sc, 0), _row(xs_ref, pad_base_ref[e] + r), sem)
                cp.wait() if wait else cp.start()
                return c
            lax.fori_loop(0, pad_len_ref[e], functools.partial(one, wait=False), 0)
            lax.fori_loop(0, pad_len_ref[e], functools.partial(one, wait=True), 0)
            return carry

        lax.fori_loop(0, N_EXPERTS, pad_rows, 0)

        def tail_tile(i, carry):
            n = tile * ROW_GROUP
            cp = pltpu.make_async_copy(zeros_sc, xs_ref.at[pl.ds(pl.multiple_of(i * n, n), n)], sem)
            cp.start()
            cp.wait()
            return carry

        lax.fori_loop(tail_ref[0], tail_ref[1], tail_tile, 0)

    copy_of = lambda row, slot: pltpu.make_async_copy(_row(x_ref, row), _row(xs_ref, pos_ref[slot * tm + row]), sem)
    _start_row_copies(tm, copy_of)
    _wait_row_copies(tm, copy_of)


def _pos_spec(tm):
    return pl.BlockSpec((2 * tm,), lambda i, *prefetch: (i,), memory_space=pltpu.SMEM)


def _token_tile_spec(rows, index_map):
    return pl.BlockSpec((rows * ROW_GROUP, LANES), index_map)


def _dispatch(plan, x1t, n_rows):
    t = x1t.shape[0] // ROW_GROUP
    tm = min(DISPATCH_TILE, t)
    grid_spec = pltpu.PrefetchScalarGridSpec(
        num_scalar_prefetch=3,
        grid=(t // tm,),
        in_specs=[_pos_spec(tm), _token_tile_spec(tm, lambda i, *prefetch: (i, 0))],
        out_specs=pl.BlockSpec(memory_space=pl.ANY),
        scratch_shapes=[pltpu.VMEM((FFN_TILE * ROW_GROUP, LANES), F32), pltpu.SemaphoreType.DMA(())],
    )
    return pl.pallas_call(
        functools.partial(_dispatch_kernel, tm=tm, tile=FFN_TILE),
        grid_spec=grid_spec,
        out_shape=jax.ShapeDtypeStruct((n_rows * ROW_GROUP, LANES), F32),
        compiler_params=_params(1),
        name="moe_dispatch",
    )(plan["pad_base"], plan["pad_len"], plan["tail"], plan["pos_dispatch"], x1t)


def _ffn_kernel(tile_expert_ref, n_used_ref, xs_ref, w1_ref, w3_ref, w2_ref, ys_ref, w1_sc, w3_sc, w2_sc):
    i = pl.program_id(0)
    used = i < n_used_ref[0]
    new_expert = jnp.logical_or(i == 0, tile_expert_ref[i] != tile_expert_ref[jnp.maximum(i - 1, 0)])

    @pl.when(jnp.logical_and(used, new_expert))
    def _():
        w1_sc[...] = w1_ref[0, 0].astype(BF16)
        w3_sc[...] = w3_ref[0, 0].astype(BF16)
        w2_sc[...] = w2_ref[0, 0].astype(BF16)

    @pl.when(used)
    def _():
        x = _load_token_tiles(xs_ref, xs_ref.shape[0] // ROW_GROUP).astype(BF16)
        a = jnp.dot(x, w1_sc[...], preferred_element_type=F32)
        b = jnp.dot(x, w3_sc[...], preferred_element_type=F32)
        h = (a * jax.nn.sigmoid(a) * b).astype(BF16)
        _store_token_tiles(ys_ref, jnp.dot(h, w2_sc[...], preferred_element_type=F32))

    @pl.when(jnp.logical_not(used))
    def _():
        ys_ref[...] = jnp.zeros(ys_ref.shape, F32)


def _expert_ffn(plan, xs, w1, w3, w2, layer):
    n_rows = xs.shape[0] // ROW_GROUP
    d, hidden = w1.shape[2], w1.shape[3]
    assert d == ROW_GROUP * LANES
    tm = FFN_TILE
    x_spec = _token_tile_spec(tm, lambda i, te, nu: (jnp.minimum(i, nu[0] - 1), 0))
    w_in = pl.BlockSpec((1, 1, d, hidden), lambda i, te, nu: (layer, te[i], 0, 0))
    w_out = pl.BlockSpec((1, 1, hidden, d), lambda i, te, nu: (layer, te[i], 0, 0))
    grid_spec = pltpu.PrefetchScalarGridSpec(
        num_scalar_prefetch=2,
        grid=(n_rows // tm,),
        in_specs=[x_spec, w_in, w_in, w_out],
        out_specs=_token_tile_spec(tm, lambda i, te, nu: (i, 0)),
        scratch_shapes=[pltpu.VMEM((d, hidden), BF16), pltpu.VMEM((d, hidden), BF16),
                        pltpu.VMEM((hidden, d), BF16)],
    )
    return pl.pallas_call(
        _ffn_kernel,
        grid_spec=grid_spec,
        out_shape=jax.ShapeDtypeStruct(xs.shape, F32),
        compiler_params=_params(1),
        name="moe_experts",
    )(plan["tile_expert"], plan["n_used"], xs, w1.astype(F32), w3.astype(F32), w2.astype(F32))


def _combine_kernel(pos_ref, pos_next_ref, x1_ref, mf_ref, ys_ref, g_ref, b_ref, out_ref, ybuf, sem, *, tm):
    i = pl.program_id(0)
    cur = lax.rem(i, 2)

    def gather(positions, buf):
        return lambda row, slot: pltpu.make_async_copy(_row(ys_ref, positions[slot * tm + row]),
                                                       _row(ybuf.at[buf, slot], row), sem.at[buf])

    @pl.when(i == 0)
    def _():
        _start_row_copies(tm, gather(pos_ref, cur))

    @pl.when(i + 1 < pl.num_programs(0))
    def _():
        _start_row_copies(tm, gather(pos_next_ref, 1 - cur))

    _wait_row_copies(tm, gather(pos_ref, cur))
    gates = mf_ref[...]
    y = (gates[:, 0:1] * _load_token_tiles(ybuf.at[cur, 0], tm)
         + gates[:, 1:2] * _load_token_tiles(ybuf.at[cur, 1], tm))
    out_ref[...] = _layer_norm(DEEPNORM_ALPHA * x1_ref[...] + y, g_ref[...], b_ref[...])


def _combine(plan, x1, meta_f, ys, ln_g, ln_b):
    t, d = x1.shape
    tm = min(COMBINE_TILE, t)
    row = lambda n: pl.BlockSpec((tm, n), lambda i: (i, 0))
    full = pl.BlockSpec((1, d), lambda i: (0, 0))
    last = t // tm - 1
    pos_next = pl.BlockSpec((2 * tm,), lambda i: (jnp.minimum(i + 1, last),), memory_space=pltpu.SMEM)
    return pl.pallas_call(
        functools.partial(_combine_kernel, tm=tm),
        grid=(t // tm,),
        in_specs=[_pos_spec(tm), pos_next, row(d), row(LANES), pl.BlockSpec(memory_space=pl.ANY), full, full],
        out_specs=row(d),
        out_shape=jax.ShapeDtypeStruct((t, d), F32),
        scratch_shapes=[pltpu.VMEM((2, 2, tm * ROW_GROUP, LANES), F32), pltpu.SemaphoreType.DMA((2,))],
        compiler_params=_params(1),
        name="moe_combine",
    )(plan["pos"], plan["pos"], x1, meta_f, ys, ln_g.reshape(1, d).astype(F32), ln_b.reshape(1, d).astype(F32))


def _routing_plan(meta_i, counts_row, n_tiles):
    counts = counts_row[0, MOE_GROUPS:MOE_GROUPS + N_EXPERTS].astype(I32)
    padded = ((counts + FFN_TILE - 1) // FFN_TILE) * FFN_TILE
    ends = jnp.cumsum(padded)
    offsets = ends - padded
    experts = jnp.arange(N_EXPERTS, dtype=I32)
    expert, rank = meta_i[0:2], meta_i[2:4]
    pos = rank + jnp.sum(jnp.where(expert[..., None] == experts, offsets, 0), axis=-1)
    tile_ends = ends // FFN_TILE
    tile_expert = jnp.sum(jnp.arange(n_tiles, dtype=I32)[:, None] >= tile_ends[None, :], axis=1)
    n_used = tile_ends[-1:].astype(I32)
    pos = pos.astype(I32)
    t = pos.shape[1]
    by_tile = lambda tm: pos.reshape(2, -1, tm).transpose(1, 0, 2).reshape(-1)
    return dict(pos=by_tile(min(COMBINE_TILE, t)), pos_dispatch=by_tile(min(DISPATCH_TILE, t)), tile_expert=jnp.minimum(tile_expert, N_EXPERTS - 1).astype(I32),
                n_used=n_used, pad_base=(offsets + counts).astype(I32), pad_len=(padded - counts).astype(I32),
                tail=jnp.concatenate([n_used, jnp.full((1,), n_tiles, I32)]))


def _hier_moe_ln(x1, x1t, meta_i, meta_f, counts_row, w1, w3, w2, layer, ln_g, ln_b):
    t, d = x1.shape
    n_tiles = (2 * t) // FFN_TILE + N_EXPERTS
    plan = _routing_plan(meta_i, counts_row, n_tiles)
    xs = _dispatch(plan, x1t, n_tiles * FFN_TILE)
    ys = _expert_ffn(plan, xs, w1, w3, w2, layer)
    return _combine(plan, x1, meta_f, ys, ln_g, ln_b)


def kernel(x, rel_table, da_wq, da_wk, da_wv, da_wo, da_lq1, da_lk1, da_lq2, da_lk2, da_subln_g, sb_wq, sb_wk,
           sb_wv, sb_wo, ln_mix_g, ln_mix_b, ln_ffn_g, ln_ffn_b, moe_w_group, moe_b_group, moe_w_expert,
           moe_b_expert, moe_w1, moe_w3, moe_w2):
    b, s, d = x.shape
    t = b * s
    assert s % DA_BLOCK == 0 and s % SB_BLOCK == 0 and (2 * t) % FFN_TILE == 0
    assert all(t % min(tile, t) == 0 for tile in (ROW_TILE, DISPATCH_TILE, COMBINE_TILE))
    xf = x.reshape(t, d).astype(F32)
    bias = _bias_tiles(rel_table, DA_BLOCK)
    for layer in range(DEPTH):
        j = layer // 2
        if layer % 2 == 0:
            q, k, v = _qkv_proj(xf, da_wq[j], da_wk[j], da_wv[j], DA_HEAD_DIM ** -0.5 * LOG2E)
            shape = (b, s, q.shape[1])
            lam_init = 0.8 - 0.6 * math.exp(-0.3 * layer)
            o = _diff_attention(q.reshape(shape), k.reshape(shape), v.reshape(shape), bias,
                                da_lq1[j], da_lk1[j], da_lq2[j], da_lk2[j], da_subln_g[j], lam_init)
            wo = da_wo[j]
        else:
            q, k, v = _qkv_proj(xf, sb_wq[j], sb_wk[j], sb_wv[j], -(SB_HEAD_DIM ** -0.5) * LOG2E)
            shape = (b, s, q.shape[1])
            o = _stick_breaking(q.reshape(shape), k.reshape(shape), v.reshape(shape))
            wo = sb_wo[j]
        x1, x1t, meta_i, meta_f, counts = _mix_ln_route(
            o.reshape(t, -1), xf, wo, ln_mix_g[layer], ln_mix_b[layer],
            moe_w_group[layer], moe_b_group[layer], moe_w_expert[layer], moe_b_expert[layer])
        xf = _hier_moe_ln(x1, x1t, meta_i, meta_f, counts, moe_w1, moe_w3, moe_w2, layer,
                          ln_ffn_g[layer], ln_ffn_b[layer])
    return xf.reshape(b, s, d).astype(x.dtype)
```

```python
import functools
import math

import jax
import jax.numpy as jnp
from jax import lax
from jax.experimental import pallas as pl
from jax.experimental.pallas import tpu as pltpu

F32 = jnp.float32
BF16 = jnp.bfloat16
I32 = jnp.int32

DEPTH = 2
DA_HEADS = 8
DA_HEAD_DIM = 64
SB_HEAD_DIM = 64
REL_BUCKETS = 32
REL_MAX_DIST = 128
MOE_GROUPS = 4
MOE_EXPERTS_PER_GROUP = 8
N_EXPERTS = MOE_GROUPS * MOE_EXPERTS_PER_GROUP
DEEPNORM_ALPHA = (2 * DEPTH) ** 0.25
LN_EPS = 1e-5
RMS_EPS = 1e-6
NEG_INF = -1e30
LOG2E = 1.4426950408889634
SB_DEAD_LOG2 = -160.0

LANES = 128
DA_BLOCK = 512
DA_PAR = 2
SB_BLOCK = 256
SB_PAR = 4
ROW_TILE = 1024
COMBINE_TILE = 512
DISPATCH_TILE = 1024
FFN_TILE = 512
META_ROWS = 8
ROW_GROUP = 8
VMEM_LIMIT = 48 * 1024 * 1024


def _params(n_axes, vmem=VMEM_LIMIT):
    return pltpu.CompilerParams(dimension_semantics=("arbitrary",) * n_axes, vmem_limit_bytes=vmem)


def _proj_kernel(x_ref, wq_ref, wk_ref, wv_ref, q_ref, k_ref, v_ref, *, q_scale):
    x = x_ref[...].astype(BF16)
    q = jnp.dot(x, wq_ref[...], preferred_element_type=F32)
    q_ref[...] = (q * q_scale).astype(BF16)
    k_ref[...] = jnp.dot(x, wk_ref[...], preferred_element_type=F32).astype(BF16)
    v_ref[...] = jnp.dot(x, wv_ref[...], preferred_element_type=F32).astype(BF16)


def _qkv_proj(x2d, wq, wk, wv, q_scale):
    t, d = x2d.shape
    n = wq.shape[1]
    tm = min(ROW_TILE, t)
    row_in = pl.BlockSpec((tm, d), lambda i: (i, 0))
    w_spec = pl.BlockSpec((d, n), lambda i: (0, 0))
    row_out = pl.BlockSpec((tm, n), lambda i: (i, 0))
    return pl.pallas_call(
        functools.partial(_proj_kernel, q_scale=q_scale),
        grid=(t // tm,),
        in_specs=[row_in, w_spec, w_spec, w_spec],
        out_specs=[row_out, row_out, row_out],
        out_shape=[jax.ShapeDtypeStruct((t, n), BF16)] * 3,
        compiler_params=_params(1),
        name="qkv_proj",
    )(x2d, wq.astype(BF16), wk.astype(BF16), wv.astype(BF16))


def _bias_kernel(tab_ref, out_ref, *, blk):
    h = pl.program_id(0)
    max_exact = REL_BUCKETS // 2
    far = tab_ref[REL_BUCKETS - 1, h]
    far_distance = _first_far_distance()
    for a in range(blk // LANES):
        for b in range(2 * blk // LANES):
            rows, cols = slice(a * LANES, (a + 1) * LANES), slice(b * LANES, (b + 1) * LANES)
            n_low = blk + a * LANES - (b * LANES + LANES - 1)
            n_high = blk + a * LANES + LANES - 1 - b * LANES
            if n_low >= far_distance:
                out_ref[0, rows, cols] = jnp.zeros((LANES, LANES), F32)
                continue
            if n_high < 0:
                out_ref[0, rows, cols] = jnp.full((LANES, LANES), NEG_INF, F32)
                continue
            i = lax.broadcasted_iota(I32, (LANES, LANES), 0)
            j = lax.broadcasted_iota(I32, (LANES, LANES), 1)
            n = n_low + (LANES - 1) + i - j
            nn = jnp.maximum(n, 0)
            nf = jnp.maximum(nn, 1).astype(F32)
            large = max_exact + (jnp.log(nf * (1.0 / max_exact))
                                 * ((REL_BUCKETS - max_exact) / math.log(REL_MAX_DIST / max_exact))).astype(I32)
            large = jnp.minimum(large, REL_BUCKETS - 1)
            bucket = jnp.where(nn < max_exact, nn, large)
            acc = jnp.zeros((LANES, LANES), F32)
            for t in range(REL_BUCKETS):
                acc = jnp.where(bucket == t, tab_ref[t, h] - far, acc)
            out_ref[0, rows, cols] = jnp.where(n >= 0, acc * LOG2E, NEG_INF)


def _bias_tiles(rel_table, blk):
    heads = rel_table.shape[1]
    return pl.pallas_call(
        functools.partial(_bias_kernel, blk=blk),
        grid=(heads,),
        in_specs=[pl.BlockSpec(memory_space=pltpu.SMEM)],
        out_specs=pl.BlockSpec((1, blk, 2 * blk), lambda h: (h, 0, 0)),
        out_shape=jax.ShapeDtypeStruct((heads, blk, 2 * blk), F32),
        compiler_params=_params(1),
        name="rel_bias_tiles",
    )(rel_table.astype(F32))


def _split_streams(q):
    lane = lax.broadcasted_iota(I32, q.shape, 1)
    zero = jnp.zeros_like(q)
    return jnp.concatenate([jnp.where(lane < 64, q, zero), jnp.where(lane >= 64, q, zero)], axis=0)


def _first_far_distance():
    max_exact = REL_BUCKETS // 2
    for n in range(max_exact, 16 * REL_MAX_DIST):
        if max_exact + int(math.log(n / max_exact) / math.log(REL_MAX_DIST / max_exact) * (REL_BUCKETS - max_exact)) \
                >= REL_BUCKETS - 1:
            return n
    raise ValueError("no distance reaches the last bucket")


NEAR_CORNER = -(-_first_far_distance() // LANES) * LANES


def _da_kernel(q_ref, k_ref, v_ref, bias_ref, lq1_ref, lk1_ref, lq2_ref, lk2_ref, g_ref, o_ref,
               m_sc, l_sc, acc_sc, s_sc, *, blk, lam_init, n_par, n_blocks):
    heads = range(n_par)
    lam = (jnp.exp(jnp.sum(lq1_ref[...] * lk1_ref[...], axis=1, keepdims=True))
           - jnp.exp(jnp.sum(lq2_ref[...] * lk2_ref[...], axis=1, keepdims=True)) + lam_init)

    def split_q(qi):
        start = pl.multiple_of(qi * blk, blk)
        return [_split_streams(q_ref[0, pl.ds(start, blk), c * LANES:(c + 1) * LANES]) for c in heads]

    def logits(qs, ki):
        start = pl.multiple_of(ki * blk, blk)
        return [lax.dot_general(qs[c], k_ref[0, pl.ds(start, blk), c * LANES:(c + 1) * LANES],
                                (((1,), (1,)), ((), ())), preferred_element_type=F32) for c in heads]

    def store_logits(vals):
        for c in heads:
            s_sc[c] = vals[c]

    def consume(ki, bias):
        start = pl.multiple_of(ki * blk, blk)
        alpha, p = [], []
        for c in heads:
            if bias == "near":
                corner = bias_ref[c, 0:NEAR_CORNER, blk - NEAR_CORNER:blk]
                for comp in range(2):
                    r0 = comp * blk
                    s_sc[c, r0:r0 + NEAR_CORNER, blk - NEAR_CORNER:blk] += corner
            sc = s_sc[c]
            if bias == "diagonal":
                tile = bias_ref[c, :, blk:2 * blk]
                sc = sc + jnp.concatenate([tile, tile], axis=0)
            m_prev = m_sc[c]
            m_next = jnp.maximum(m_prev, jnp.max(sc, axis=1, keepdims=True))
            m_sc[c] = m_next
            alpha.append(jnp.exp2(m_prev - m_next))
            pc = jnp.exp2(sc - jnp.concatenate([m_next] * (blk // LANES), axis=1))
            l_sc[c] = alpha[c] * l_sc[c] + sum(pc[:, n * LANES:(n + 1) * LANES] for n in range(blk // LANES))
            p.append(pc.astype(BF16))
        pv = [jnp.dot(p[c], v_ref[0, pl.ds(start, blk), c * LANES:(c + 1) * LANES],
                      preferred_element_type=F32) for c in heads]
        for c in heads:
            acc_sc[c] = alpha[c] * acc_sc[c] + pv[c]

    store_logits(logits(split_q(0), 0))

    def query_block(qi, carry):
        qs = split_q(qi)
        m_sc[...] = jnp.full(m_sc.shape, NEG_INF, F32)
        l_sc[...] = jnp.zeros(l_sc.shape, F32)
        acc_sc[...] = jnp.zeros(acc_sc.shape, F32)

        def step(ki, bias):
            nxt = logits(qs, ki + 1)
            consume(ki, bias)
            store_logits(nxt)

        def far_body(ki, c):
            step(ki, None)
            return c

        lax.fori_loop(0, jnp.maximum(qi - 1, 0), far_body, 0)

        @pl.when(qi >= 1)
        def _():
            step(qi - 1, "near")

        nxt = logits(split_q(jnp.minimum(qi + 1, n_blocks - 1)), 0)
        consume(qi, "diagonal")
        store_logits(nxt)

        rows = pl.ds(pl.multiple_of(qi * blk, blk), blk)
        for c in heads:
            inv_l = 1.0 / jnp.sum(l_sc[c], axis=1, keepdims=True)
            acc = acc_sc[c]
            o = acc[:blk] * inv_l[:blk] - lam * (acc[blk:] * inv_l[blk:])
            o = o * lax.rsqrt(jnp.mean(o * o, axis=1, keepdims=True) + RMS_EPS) * g_ref[...]
            o_ref[0, rows, c * LANES:(c + 1) * LANES] = (o * (1.0 - lam_init)).astype(BF16)
        return carry

    lax.fori_loop(0, n_blocks, query_block, 0)


def _diff_attention(q, k, v, bias, lq1, lk1, lq2, lk2, subln_g, lam_init):
    b, s, _ = q.shape
    blk, n_par = DA_BLOCK, DA_PAR
    assert NEAR_CORNER <= blk
    seq_spec = pl.BlockSpec((1, s, n_par * LANES), lambda bi, h: (bi, 0, h))
    vec64 = pl.BlockSpec((1, DA_HEAD_DIM), lambda bi, h: (0, 0))
    state = pltpu.VMEM((n_par, 2 * blk, LANES), F32)
    return pl.pallas_call(
        functools.partial(_da_kernel, blk=blk, lam_init=lam_init, n_par=n_par, n_blocks=s // blk),
        grid=(b, DA_HEADS // n_par),
        in_specs=[seq_spec, seq_spec, seq_spec,
                  pl.BlockSpec((n_par, blk, 2 * blk), lambda bi, h: (h, 0, 0)),
                  vec64, vec64, vec64, vec64,
                  pl.BlockSpec((1, LANES), lambda bi, h: (0, 0))],
        out_specs=seq_spec,
        out_shape=jax.ShapeDtypeStruct(q.shape, BF16),
        scratch_shapes=[state, state, state, pltpu.VMEM((n_par, 2 * blk, blk), F32)],
        compiler_params=_params(2),
        name="diff_attention",
    )(q, k, v, bias, lq1.reshape(1, -1), lk1.reshape(1, -1), lq2.reshape(1, -1), lk2.reshape(1, -1),
      subln_g.reshape(1, -1))


def _sb_kernel(q_ref, k_ref, v_ref, o_ref, later2_sc, r_sc, acc_sc, *, blk, n_par, n_blocks):
    rows = 2 * blk
    jj = lax.broadcasted_iota(I32, (blk, blk), 0)
    ss = lax.broadcasted_iota(I32, (blk, blk), 1)
    later = jnp.where(jj > ss, 1.0, 0.0).astype(BF16)
    later2_sc[...] = jnp.concatenate([later, later], axis=0)
    lane = lax.broadcasted_iota(I32, (blk, LANES), 1)
    lax.fori_loop(0, n_blocks, functools.partial(_sb_query_block, q_ref, k_ref, v_ref, o_ref, later2_sc, r_sc,
                                                 acc_sc, lane, blk, n_par, rows), 0)


def _sb_query_block(q_ref, k_ref, v_ref, o_ref, later2_sc, r_sc, acc_sc, lane, blk, n_par, rows, qi, carry):
    q_rows = pl.ds(pl.multiple_of(qi * blk, blk), blk)
    qs = [_split_streams(q_ref[0, q_rows, c * LANES:(c + 1) * LANES]) for c in range(n_par)]
    later2 = later2_sc[...]
    r_sc[...] = jnp.zeros(r_sc.shape, F32)
    acc_sc[...] = jnp.zeros(acc_sc.shape, F32)

    def blocks(ki, diagonal):
        start = pl.multiple_of(ki * blk, blk)
        pairs = range(n_par)
        zn = [lax.dot_general(qs[c], k_ref[0, pl.ds(start, blk), c * LANES:(c + 1) * LANES],
                              (((1,), (1,)), ((), ())), preferred_element_type=F32) for c in pairs]
        if diagonal:
            r = lax.broadcasted_iota(I32, (rows, blk), 0)
            t = jnp.where(r >= blk, r - blk, r)
            strict = lax.broadcasted_iota(I32, (rows, blk), 1) < t
        log_fail, log_beta, hi_lo = [], [], []
        for c in pairs:
            neg_abs = pltpu.bitcast(pltpu.bitcast(zn[c], jnp.uint32) | jnp.uint32(0x80000000), F32)
            softplus = jnp.log(1.0 + jnp.exp2(neg_abs)) * LOG2E
            lf = jnp.minimum(zn[c], 0.0) - softplus
            log_beta.append(lf - zn[c])
            if diagonal:
                lf = jnp.where(strict, lf, 0.0)
            log_fail.append(lf)
            hi = pltpu.bitcast(pltpu.bitcast(lf, jnp.uint32) & jnp.uint32(0xFFFF0000), F32)
            hi_lo.append(jnp.concatenate([hi.astype(BF16), (lf - hi).astype(BF16)], axis=1))
        suffix = [jnp.dot(hi_lo[c], later2, preferred_element_type=F32) for c in pairs]
        wb = []
        for c in pairs:
            w = jnp.exp2(log_beta[c] + suffix[c])
            if diagonal:
                w = jnp.where(strict, w, 0.0)
            wb.append(w.astype(BF16))
        pv = []
        for c in pairs:
            vb = v_ref[0, pl.ds(start, blk), c * LANES:(c + 1) * LANES]
            pv.append((jnp.dot(wb[c][:blk], vb, preferred_element_type=F32),
                       jnp.dot(wb[c][blk:], vb, preferred_element_type=F32)))
        for c in pairs:
            carry = jnp.exp2(r_sc[c])
            acc_sc[c] += jnp.where(lane < 64, carry[:blk] * pv[c][0], carry[blk:] * pv[c][1])
            r_sc[c] += suffix[c][:, 0:1] + log_fail[c][:, 0:1]

    blocks(qi, True)

    def alive():
        return jnp.max(r_sc[...]) > SB_DEAD_LOG2

    def cond(carry):
        step, live = carry
        return jnp.logical_and(step < qi, live)

    def body(carry):
        step, _ = carry
        blocks(qi - 1 - step, False)
        return step + 1, alive()

    lax.while_loop(cond, body, (jnp.int32(0), alive()))
    for c in range(n_par):
        o_ref[0, q_rows, c * LANES:(c + 1) * LANES] = acc_sc[c].astype(BF16)
    return carry


def _stick_breaking(q, k, v):
    b, s, width = q.shape
    blk, n_par = SB_BLOCK, SB_PAR
    seq_spec = pl.BlockSpec((1, s, n_par * LANES), lambda bi, h: (bi, 0, h))
    return pl.pallas_call(
        functools.partial(_sb_kernel, blk=blk, n_par=n_par, n_blocks=s // blk),
        grid=(b, width // (n_par * LANES)),
        in_specs=[seq_spec, seq_spec, seq_spec],
        out_specs=seq_spec,
        out_shape=jax.ShapeDtypeStruct(q.shape, BF16),
        scratch_shapes=[pltpu.VMEM((2 * blk, blk), BF16), pltpu.VMEM((n_par, 2 * blk, LANES), F32),
                        pltpu.VMEM((n_par, blk, LANES), F32)],
        compiler_params=_params(2),
        name="stick_breaking",
    )(q, k, v)


def _layer_norm(y, g, b):
    mu = jnp.mean(y, axis=1, keepdims=True)
    yc = y - mu
    var = jnp.mean(yc * yc, axis=1, keepdims=True)
    return yc * lax.rsqrt(var + LN_EPS) * g + b


def _first_index_of_max(vals, lane_f):
    m = jnp.max(vals, axis=1, keepdims=True)
    idx = jnp.min(jnp.where(vals == m, lane_f, float(LANES)), axis=1, keepdims=True)
    return m, idx


def _top_experts(logits):
    lane_f = lax.broadcasted_iota(I32, logits.shape, 1).astype(F32)
    is_group = lane_f < MOE_GROUPS
    g_max, g_idx = _first_index_of_max(jnp.where(is_group, logits, NEG_INF), lane_f)
    g_gate = 1.0 / jnp.sum(jnp.where(is_group, jnp.exp(logits - g_max), 0.0), axis=1, keepdims=True)
    first = MOE_GROUPS + MOE_EXPERTS_PER_GROUP * g_idx
    in_group = (lane_f >= first) & (lane_f < first + MOE_EXPERTS_PER_GROUP)
    e_logits = jnp.where(in_group, logits, NEG_INF)
    m1, i1 = _first_index_of_max(e_logits, lane_f)
    m2, i2 = _first_index_of_max(jnp.where(lane_f == i1, NEG_INF, e_logits), lane_f)
    ratio = jnp.exp(m2 - m1)
    w1 = g_gate / (1.0 + ratio)
    return i1, i2, w1, w1 * ratio


def _mix_ln_route_kernel(o_ref, x_ref, wo_ref, g_ref, b_ref, wr_hi_ref, wr_lo_ref, br_ref,
                         x1_ref, x1t_ref, mi_ref, mf_ref, cnt_ref, *, tm):
    step = pl.program_id(0)

    @pl.when(step == 0)
    def _():
        cnt_ref[...] = jnp.zeros(cnt_ref.shape, F32)

    mix = jnp.dot(o_ref[...], wo_ref[...], preferred_element_type=F32)
    x1 = _layer_norm(DEEPNORM_ALPHA * x_ref[...] + mix, g_ref[...], b_ref[...])
    x1_ref[...] = x1
    _store_token_tiles(x1t_ref, x1)

    xh = x1.astype(BF16)
    xl = (x1 - xh.astype(F32)).astype(BF16)
    logits = (jnp.dot(xh, wr_hi_ref[...], preferred_element_type=F32)
              + jnp.dot(xh, wr_lo_ref[...], preferred_element_type=F32)
              + jnp.dot(xl, wr_hi_ref[...], preferred_element_type=F32)) + br_ref[...]
    i1, i2, w1, w2 = _top_experts(logits)

    lane = lax.broadcasted_iota(I32, (tm, LANES), 1)
    lane_f = lane.astype(F32)
    e1 = jnp.where(lane_f == i1, 1.0, 0.0)
    e2 = jnp.where(lane_f == i2, 1.0, 0.0)
    both = e1 + e2
    rr = lax.broadcasted_iota(I32, (tm, tm), 0)
    cc = lax.broadcasted_iota(I32, (tm, tm), 1)
    earlier = jnp.where(rr > cc, 1.0, 0.0).astype(BF16)
    before = cnt_ref[...] + jnp.dot(earlier, both.astype(BF16), preferred_element_type=F32)
    rank1 = jnp.sum(e1 * before, axis=1, keepdims=True)
    rank2 = jnp.sum(e2 * before, axis=1, keepdims=True)
    cnt_ref[...] += jnp.sum(both, axis=0, keepdims=True)

    meta = jnp.where(lane == 0, i1 - MOE_GROUPS,
                     jnp.where(lane == 1, i2 - MOE_GROUPS,
                               jnp.where(lane == 2, rank1, jnp.where(lane == 3, rank2, 0.0))))
    mi_ref[...] = meta.T[:META_ROWS].astype(I32)
    mf_ref[...] = jnp.where(lane == 0, w1, jnp.where(lane == 1, w2, 0.0))


def _mix_ln_route(o2d, x2d, wo, ln_g, ln_b, w_group, b_group, w_expert, b_expert):
    t, d = x2d.shape
    tm = min(ROW_TILE, t)
    pad = LANES - MOE_GROUPS - N_EXPERTS
    wr = jnp.pad(jnp.concatenate([w_group, w_expert], axis=1).astype(F32), ((0, 0), (0, pad)))
    wr_hi = wr.astype(BF16)
    wr_lo = (wr - wr_hi.astype(F32)).astype(BF16)
    br = jnp.pad(jnp.concatenate([b_group, b_expert]).astype(F32), (0, pad)).reshape(1, LANES)
    row = lambda n: pl.BlockSpec((tm, n), lambda i: (i, 0))
    full = lambda r, c: pl.BlockSpec((r, c), lambda i: (0, 0))
    return pl.pallas_call(
        functools.partial(_mix_ln_route_kernel, tm=tm),
        grid=(t // tm,),
        in_specs=[row(o2d.shape[1]), row(d), full(o2d.shape[1], d), full(1, d), full(1, d),
                  full(d, LANES), full(d, LANES), full(1, LANES)],
        out_specs=[row(d), pl.BlockSpec((tm * ROW_GROUP, LANES), lambda i: (i, 0)),
                   pl.BlockSpec((META_ROWS, tm), lambda i: (0, i)), row(LANES), full(1, LANES)],
        out_shape=[jax.ShapeDtypeStruct((t, d), F32), jax.ShapeDtypeStruct((t * ROW_GROUP, LANES), F32),
                   jax.ShapeDtypeStruct((META_ROWS, t), I32),
                   jax.ShapeDtypeStruct((t, LANES), F32), jax.ShapeDtypeStruct((1, LANES), F32)],
        compiler_params=_params(1),
        name="mix_ln_route",
    )(o2d, x2d, wo.astype(BF16), ln_g.reshape(1, d).astype(F32), ln_b.reshape(1, d).astype(F32),
      wr_hi, wr_lo, br)


def _load_token_tiles(ref, rows):
    return jnp.concatenate([ref[pl.ds(s, rows, stride=ROW_GROUP), :] for s in range(ROW_GROUP)], axis=1)


def _store_token_tiles(ref, val):
    for s in range(ROW_GROUP):
        ref[pl.ds(s, val.shape[0], stride=ROW_GROUP), :] = val[:, s * LANES:(s + 1) * LANES]


def _start_row_copies(tm, copy_of):
    def group(g, carry):
        for r in range(ROW_GROUP):
            for slot in range(2):
                copy_of(g * ROW_GROUP + r, slot).start(priority=slot)
        return carry

    lax.fori_loop(0, tm // ROW_GROUP, group, 0)


def _wait_row_copies(tm, copy_of):
    def group(g, carry):
        for _ in range(2 * ROW_GROUP):
            copy_of(0, 0).wait()
        return carry

    lax.fori_loop(0, tm // ROW_GROUP, group, 0)


def _row(ref, row):
    return ref.at[pl.ds(pl.multiple_of(row * ROW_GROUP, ROW_GROUP), ROW_GROUP)]


def _dispatch_kernel(pad_base_ref, pad_len_ref, tail_ref, pos_ref, x_ref, xs_ref, zeros_sc, sem, *, tm, tile):
    @pl.when(pl.program_id(0) == 0)
    def _():
        zeros_sc[...] = jnp.zeros(zeros_sc.shape, F32)

        def pad_rows(e, carry):
            def one(r, c, wait):
                cp = pltpu.make_async_copy(_row(zeros_sc, 0), _row(xs_ref, pad_base_ref[e] + r), sem)
                cp.wait() if wait else cp.start()
                return c
            lax.fori_loop(0, pad_len_ref[e], functools.partial(one, wait=False), 0)
            lax.fori_loop(0, pad_len_ref[e], functools.partial(one, wait=True), 0)
            return carry

        lax.fori_loop(0, N_EXPERTS, pad_rows, 0)

        def tail_tile(i, carry):
            n = tile * ROW_GROUP
            cp = pltpu.make_async_copy(zeros_sc, xs_ref.at[pl.ds(pl.multiple_of(i * n, n), n)], sem)
            cp.start()
            cp.wait()
            return carry

        lax.fori_loop(tail_ref[0], tail_ref[1], tail_tile, 0)

    copy_of = lambda row, slot: pltpu.make_async_copy(_row(x_ref, row), _row(xs_ref, pos_ref[slot * tm + row]), sem)
    _start_row_copies(tm, copy_of)
    _wait_row_copies(tm, copy_of)


def _pos_spec(tm):
    return pl.BlockSpec((2 * tm,), lambda i, *prefetch: (i,), memory_space=pltpu.SMEM)


def _token_tile_spec(rows, index_map):
    return pl.BlockSpec((rows * ROW_GROUP, LANES), index_map)


def _dispatch(plan, x1t, n_rows):
    t = x1t.shape[0] // ROW_GROUP
    tm = min(DISPATCH_TILE, t)
    grid_spec = pltpu.PrefetchScalarGridSpec(
        num_scalar_prefetch=3,
        grid=(t // tm,),
        in_specs=[_pos_spec(tm), _token_tile_spec(tm, lambda i, *prefetch: (i, 0))],
        out_specs=pl.BlockSpec(memory_space=pl.ANY),
        scratch_shapes=[pltpu.VMEM((FFN_TILE * ROW_GROUP, LANES), F32), pltpu.SemaphoreType.DMA(())],
    )
    return pl.pallas_call(
        functools.partial(_dispatch_kernel, tm=tm, tile=FFN_TILE),
        grid_spec=grid_spec,
        out_shape=jax.ShapeDtypeStruct((n_rows * ROW_GROUP, LANES), F32),
        compiler_params=_params(1),
        name="moe_dispatch",
    )(plan["pad_base"], plan["pad_len"], plan["tail"], plan["pos_dispatch"], x1t)


def _ffn_kernel(tile_expert_ref, n_used_ref, xs_ref, w1_ref, w3_ref, w2_ref, ys_ref, w1_sc, w3_sc, w2_sc):
    i = pl.program_id(0)
    used = i < n_used_ref[0]
    new_expert = jnp.logical_or(i == 0, tile_expert_ref[i] != tile_expert_ref[jnp.maximum(i - 1, 0)])

    @pl.when(jnp.logical_and(used, new_expert))
    def _():
        w1_sc[...] = w1_ref[0, 0].astype(BF16)
        w3_sc[...] = w3_ref[0, 0].astype(BF16)
        w2_sc[...] = w2_ref[0, 0].astype(BF16)

    @pl.when(used)
    def _():
        x = _load_token_tiles(xs_ref, xs_ref.shape[0] // ROW_GROUP).astype(BF16)
        a = jnp.dot(x, w1_sc[...], preferred_element_type=F32)
        b = jnp.dot(x, w3_sc[...], preferred_element_type=F32)
        h = (a * jax.nn.sigmoid(a) * b).astype(BF16)
        _store_token_tiles(ys_ref, jnp.dot(h, w2_sc[...], preferred_element_type=F32))

    @pl.when(jnp.logical_not(used))
    def _():
        ys_ref[...] = jnp.zeros(ys_ref.shape, F32)


def _expert_ffn(plan, xs, w1, w3, w2, layer):
    n_rows = xs.shape[0] // ROW_GROUP
    d, hidden = w1.shape[2], w1.shape[3]
    assert d == ROW_GROUP * LANES
    tm = FFN_TILE
    x_spec = _token_tile_spec(tm, lambda i, te, nu: (jnp.minimum(i, nu[0] - 1), 0))
    w_in = pl.BlockSpec((1, 1, d, hidden), lambda i, te, nu: (layer, te[i], 0, 0))
    w_out = pl.BlockSpec((1, 1, hidden, d), lambda i, te, nu: (layer, te[i], 0, 0))
    grid_spec = pltpu.PrefetchScalarGridSpec(
        num_scalar_prefetch=2,
        grid=(n_rows // tm,),
        in_specs=[x_spec, w_in, w_in, w_out],
        out_specs=_token_tile_spec(tm, lambda i, te, nu: (i, 0)),
        scratch_shapes=[pltpu.VMEM((d, hidden), BF16), pltpu.VMEM((d, hidden), BF16),
                        pltpu.VMEM((hidden, d), BF16)],
    )
    return pl.pallas_call(
        _ffn_kernel,
        grid_spec=grid_spec,
        out_shape=jax.ShapeDtypeStruct(xs.shape, F32),
        compiler_params=_params(1),
        name="moe_experts",
    )(plan["tile_expert"], plan["n_used"], xs, w1.astype(F32), w3.astype(F32), w2.astype(F32))


def _combine_kernel(pos_ref, pos_next_ref, x1_ref, mf_ref, ys_ref, g_ref, b_ref, out_ref, ybuf, sem, *, tm):
    i = pl.program_id(0)
    cur = lax.rem(i, 2)

    def gather(positions, buf):
        return lambda row, slot: pltpu.make_async_copy(_row(ys_ref, positions[slot * tm + row]),
                                                       _row(ybuf.at[buf, slot], row), sem.at[buf])

    @pl.when(i == 0)
    def _():
        _start_row_copies(tm, gather(pos_ref, cur))

    @pl.when(i + 1 < pl.num_programs(0))
    def _():
        _start_row_copies(tm, gather(pos_next_ref, 1 - cur))

    _wait_row_copies(tm, gather(pos_ref, cur))
    gates = mf_ref[...]
    y = (gates[:, 0:1] * _load_token_tiles(ybuf.at[cur, 0], tm)
         + gates[:, 1:2] * _load_token_tiles(ybuf.at[cur, 1], tm))
    out_ref[...] = _layer_norm(DEEPNORM_ALPHA * x1_ref[...] + y, g_ref[...], b_ref[...])


def _combine(plan, x1, meta_f, ys, ln_g, ln_b):
    t, d = x1.shape
    tm = min(COMBINE_TILE, t)
    row = lambda n: pl.BlockSpec((tm, n), lambda i: (i, 0))
    full = pl.BlockSpec((1, d), lambda i: (0, 0))
    last = t // tm - 1
    pos_next = pl.BlockSpec((2 * tm,), lambda i: (jnp.minimum(i + 1, last),), memory_space=pltpu.SMEM)
    return pl.pallas_call(
        functools.partial(_combine_kernel, tm=tm),
        grid=(t // tm,),
        in_specs=[_pos_spec(tm), pos_next, row(d), row(LANES), pl.BlockSpec(memory_space=pl.ANY), full, full],
        out_specs=row(d),
        out_shape=jax.ShapeDtypeStruct((t, d), F32),
        scratch_shapes=[pltpu.VMEM((2, 2, tm * ROW_GROUP, LANES), F32), pltpu.SemaphoreType.DMA((2,))],
        compiler_params=_params(1),
        name="moe_combine",
    )(plan["pos"], plan["pos"], x1, meta_f, ys, ln_g.reshape(1, d).astype(F32), ln_b.reshape(1, d).astype(F32))


def _routing_plan(meta_i, counts_row, n_tiles):
    counts = counts_row[0, MOE_GROUPS:MOE_GROUPS + N_EXPERTS].astype(I32)
    padded = ((counts + FFN_TILE - 1) // FFN_TILE) * FFN_TILE
    ends = jnp.cumsum(padded)
    offsets = ends - padded
    experts = jnp.arange(N_EXPERTS, dtype=I32)
    expert, rank = meta_i[0:2], meta_i[2:4]
    pos = rank + jnp.sum(jnp.where(expert[..., None] == experts, offsets, 0), axis=-1)
    tile_ends = ends // FFN_TILE
    tile_expert = jnp.sum(jnp.arange(n_tiles, dtype=I32)[:, None] >= tile_ends[None, :], axis=1)
    n_used = tile_ends[-1:].astype(I32)
    pos = pos.astype(I32)
    t = pos.shape[1]
    by_tile = lambda tm: pos.reshape(2, -1, tm).transpose(1, 0, 2).reshape(-1)
    return dict(pos=by_tile(min(COMBINE_TILE, t)), pos_dispatch=by_tile(min(DISPATCH_TILE, t)), tile_expert=jnp.minimum(tile_expert, N_EXPERTS - 1).astype(I32),
                n_used=n_used, pad_base=(offsets + counts).astype(I32), pad_len=(padded - counts).astype(I32),
                tail=jnp.concatenate([n_used, jnp.full((1,), n_tiles, I32)]))


def _hier_moe_ln(x1, x1t, meta_i, meta_f, counts_row, w1, w3, w2, layer, ln_g, ln_b):
    t, d = x1.shape
    n_tiles = (2 * t) // FFN_TILE + N_EXPERTS
    plan = _routing_plan(meta_i, counts_row, n_tiles)
    xs = _dispatch(plan, x1t, n_tiles * FFN_TILE)
    ys = _expert_ffn(plan, xs, w1, w3, w2, layer)
    return _combine(plan, x1, meta_f, ys, ln_g, ln_b)


def kernel(x, rel_table, da_wq, da_wk, da_wv, da_wo, da_lq1, da_lk1, da_lq2, da_lk2, da_subln_g, sb_wq, sb_wk,
           sb_wv, sb_wo, ln_mix_g, ln_mix_b, ln_ffn_g, ln_ffn_b, moe_w_group, moe_b_group, moe_w_expert,
           moe_b_expert, moe_w1, moe_w3, moe_w2):
    b, s, d = x.shape
    t = b * s
    assert s % DA_BLOCK == 0 and s % SB_BLOCK == 0 and (2 * t) % FFN_TILE == 0
    assert all(t % min(tile, t) == 0 for tile in (ROW_TILE, DISPATCH_TILE, COMBINE_TILE))
    xf = x.reshape(t, d).astype(F32)
    bias = _bias_tiles(rel_table, DA_BLOCK)
    for layer in range(DEPTH):
        j = layer // 2
        if layer % 2 == 0:
            q, k, v = _qkv_proj(xf, da_wq[j], da_wk[j], da_wv[j], DA_HEAD_DIM ** -0.5 * LOG2E)
            shape = (b, s, q.shape[1])
            lam_init = 0.8 - 0.6 * math.exp(-0.3 * layer)
            o = _diff_attention(q.reshape(shape), k.reshape(shape), v.reshape(shape), bias,
                                da_lq1[j], da_lk1[j], da_lq2[j], da_lk2[j], da_subln_g[j], lam_init)
            wo = da_wo[j]
        else:
            q, k, v = _qkv_proj(xf, sb_wq[j], sb_wk[j], sb_wv[j], -(SB_HEAD_DIM ** -0.5) * LOG2E)
            shape = (b, s, q.shape[1])
            o = _stick_breaking(q.reshape(shape), k.reshape(shape), v.reshape(shape))
            wo = sb_wo[j]
        x1, x1t, meta_i, meta_f, counts = _mix_ln_route(
            o.reshape(t, -1), xf, wo, ln_mix_g[layer], ln_mix_b[layer],
            moe_w_group[layer], moe_b_group[layer], moe_w_expert[layer], moe_b_expert[layer])
        xf = _hier_moe_ln(x1, x1t, meta_i, meta_f, counts, moe_w1, moe_w3, moe_w2, layer,
                          ln_ffn_g[layer], ln_ffn_b[layer])
    return xf.reshape(b, s, d).astype(x.dtype)
```
